```python
import jax
import jax.numpy as jnp
from jax import lax
import numpy as np


D_MODEL = 2048
BATCH = 2
SEQ = 4096
DEPTH = 2

HEAD_DIM = 128
ATT_HEADS = D_MODEL // 256
CONV_GROUPS = D_MODEL // 512
HGRN_HEADS = D_MODEL // 512
ATT_WIDTH = ATT_HEADS * HEAD_DIM
CONV_WIDTH = CONV_GROUPS * HEAD_DIM
HGRN_WIDTH = HGRN_HEADS * HEAD_DIM
D_MIX = ATT_WIDTH + CONV_WIDTH + HGRN_WIDTH
CONV_K = 3
Q_BLOCK = 128
HGRN_CHUNK = 16
N_EXPERTS = 32
TOP_K = 4
D_FF = D_MODEL
SWIGLU_LIMIT = 7.0
SWIGLU_ALPHA = 1.702
DEEPNORM_ALPHA = (2 * DEPTH) ** 0.25
DEEPNORM_BETA = (8 * DEPTH) ** -0.25
LN_EPS = 1e-5
RMS_EPS = 1e-6
SPLITS = (ATT_WIDTH, ATT_WIDTH, ATT_WIDTH, ATT_HEADS,
          CONV_WIDTH, CONV_WIDTH, CONV_WIDTH,
          HGRN_WIDTH, HGRN_WIDTH, HGRN_WIDTH, HGRN_WIDTH)
VALUE_SPLITS = (2, 6, 9)
SPLIT_IDX = [int(v) for v in np.cumsum(SPLITS)[:-1]]
D_IN_PROJ = int(sum(SPLITS))

kernel_name = 'hybrid_fox_conv_hgrn2_moe_deepnorm'


def layer_norm(x, g, b):
    xf = x.astype(jnp.float32)
    mu = jnp.mean(xf, axis=-1, keepdims=True)
    var = jnp.mean(jnp.square(xf - mu), axis=-1, keepdims=True)
    return ((xf - mu) * lax.rsqrt(var + LN_EPS) * g + b).astype(x.dtype)


def head_rms_norm(o, g):
    b, s, w = o.shape
    of = o.astype(jnp.float32).reshape(b, s, w // HEAD_DIM, HEAD_DIM)
    of = of * lax.rsqrt(jnp.mean(of * of, axis=-1, keepdims=True) + RMS_EPS)
    return (of.reshape(b, s, w) * g).astype(o.dtype)


def forgetting_attention(q, k, v, f_logit):
    b, s, _ = q.shape
    nb = s // Q_BLOCK
    to_heads = lambda t: t.reshape(b, s, ATT_HEADS, HEAD_DIM).transpose(0, 2, 1, 3)
    q, k, v = to_heads(q), to_heads(k), to_heads(v)
    c = jnp.cumsum(jax.nn.log_sigmoid(f_logit.astype(jnp.float32)), axis=1).transpose(0, 2, 1)
    q_blocks = q.reshape(b, ATT_HEADS, nb, Q_BLOCK, HEAD_DIM).transpose(2, 0, 1, 3, 4)
    c_blocks = c.reshape(b, ATT_HEADS, nb, Q_BLOCK).transpose(2, 0, 1, 3)
    k_pos = jnp.arange(s)
    scale = HEAD_DIM ** -0.5

    def block(args):
        q_blk, c_blk, i = args
        q_pos = i * Q_BLOCK + jnp.arange(Q_BLOCK)
        logits = jnp.einsum('bhqd,bhkd->bhqk', q_blk, k, preferred_element_type=jnp.float32) * scale
        logits = logits + (c_blk[..., :, None] - c[..., None, :])
        logits = jnp.where(k_pos[None, :] <= q_pos[:, None], logits, -jnp.inf)
        p = jax.nn.softmax(logits, axis=-1)
        return jnp.einsum('bhqk,bhkd->bhqd', p.astype(v.dtype), v)

    o = lax.map(block, (q_blocks, c_blocks, jnp.arange(nb)))
    return o.transpose(1, 0, 3, 2, 4).reshape(b, s, ATT_WIDTH)


def short_conv_mixer(gate_b, gate_c, h, conv_w):
    u = gate_c * h
    y = lax.conv_general_dilated(u, conv_w[:, None, :], window_strides=(1,),
                                 padding=[(CONV_K - 1, 0)],
                                 dimension_numbers=('NWC', 'WIO', 'NWC'),
                                 feature_group_count=CONV_WIDTH)
    return gate_b * y


def hgrn2_mixer(q, f_logit, i, lower_bound):
    b, s, _ = q.shape
    n = s // HGRN_CHUNK
    to_heads = lambda t: t.reshape(b, n, HGRN_CHUNK, HGRN_HEADS, HEAD_DIM).transpose(0, 3, 1, 2, 4)
    zf = f_logit.astype(jnp.float32)
    log_f = jnp.logaddexp(jnp.log(lower_bound), jnp.log1p(-lower_bound) + jax.nn.log_sigmoid(zf))
    k = -jnp.expm1(log_f)
    qh = to_heads(jax.nn.silu(q.astype(jnp.float32)))
    kh = to_heads(k)
    vh = to_heads(i.astype(jnp.float32))
    bcum = jnp.cumsum(to_heads(log_f), axis=3)
    causal = jnp.tril(jnp.ones((HGRN_CHUNK, HGRN_CHUNK), dtype=bool))
    decay = jnp.exp(jnp.where(causal[:, :, None],
                              bcum[..., :, None, :] - bcum[..., None, :, :], -jnp.inf))
    scores = jnp.einsum('bhntd,bhnsd,bhntsd->bhnts', qh, kh, decay)
    o_intra = jnp.einsum('bhnts,bhnsv->bhntv', scores, vh)
    b_last = bcum[..., -1, :]
    kv = jnp.einsum('bhnsd,bhnsv->bhndv', kh * jnp.exp(b_last[..., None, :] - bcum), vh)
    chunk_decay = jnp.exp(b_last)

    def step(state, inp):
        dec, kv_c = inp
        return dec[..., None] * state + kv_c, state

    init = jnp.zeros((b, HGRN_HEADS, HEAD_DIM, HEAD_DIM), jnp.float32)
    _, s_prev = lax.scan(step, init, (jnp.moveaxis(chunk_decay, 2, 0), jnp.moveaxis(kv, 2, 0)))
    s_prev = jnp.moveaxis(s_prev, 0, 2)
    o_inter = jnp.einsum('bhntd,bhndv->bhntv', qh * jnp.exp(bcum), s_prev)
    o = (o_intra + o_inter).transpose(0, 2, 3, 1, 4).reshape(b, s, HGRN_WIDTH)
    return o.astype(q.dtype)


def hybrid_mixer(x, w_in, b_fgate, conv_w, lower_bound, head_norm_g, w_out):
    proj = x @ w_in
    (a_q, a_k, a_v, a_f, c_b, c_c, c_h, h_q, h_f, h_i, h_g) = jnp.split(proj, SPLIT_IDX, axis=-1)
    o_att = forgetting_attention(a_q, a_k, a_v, a_f + b_fgate)
    o_conv = short_conv_mixer(c_b, c_c, c_h, conv_w)
    o_rec = hgrn2_mixer(h_q, h_f, h_i, lower_bound)
    g_att, g_conv, g_rec = jnp.split(head_norm_g, [ATT_WIDTH, ATT_WIDTH + CONV_WIDTH])
    o = jnp.concatenate([head_rms_norm(o_att, g_att),
                         head_rms_norm(o_conv, g_conv),
                         head_rms_norm(o_rec, g_rec) * jax.nn.silu(h_g)], axis=-1)
    return o @ w_out


def moe_ffn(x, w_router, b_router, w1, b1, w2, b2):
    b, s, d = x.shape
    xt = x.reshape(b * s, d)
    logits = (xt @ w_router + b_router).astype(jnp.float32)
    top_val, top_idx = lax.top_k(logits, TOP_K)
    top_w = jax.nn.softmax(top_val, axis=-1)
    combine = jnp.einsum('tk,tke->te', top_w, jax.nn.one_hot(top_idx, N_EXPERTS, dtype=jnp.float32))
    y = jnp.zeros((b * s, d), jnp.float32)
    for e in range(N_EXPERTS):
        h = xt @ w1[e] + b1[e]
        gate, up = jnp.split(h, 2, axis=-1)
        gate = jnp.minimum(gate, SWIGLU_LIMIT)
        up = jnp.clip(up, -SWIGLU_LIMIT, SWIGLU_LIMIT)
        act = (up + 1.0) * gate * jax.nn.sigmoid(SWIGLU_ALPHA * gate)
        y = y + combine[:, e:e + 1] * (act @ w2[e] + b2[e])
    return y.reshape(b, s, d).astype(x.dtype)


def setup_inputs(seed: int = 0) -> dict:
    key = jax.random.key(seed)
    ks = jax.random.split(key, 17)
    f32 = jnp.float32

    def nrm(k, shape, scale):
        return jax.random.normal(k, shape, f32) * scale

    col_scale = jnp.concatenate([jnp.full((n,), DEEPNORM_BETA if j in VALUE_SPLITS else 1.0, f32)
                                 for j, n in enumerate(SPLITS)])
    return {
        'x': nrm(ks[0], (BATCH, SEQ, D_MODEL), 1.0),
        'w_in': nrm(ks[1], (DEPTH, D_MODEL, D_IN_PROJ), D_MODEL ** -0.5) * col_scale,
        'b_fgate': nrm(ks[2], (DEPTH, ATT_HEADS), 0.1),
        'conv_w': nrm(ks[3], (DEPTH, CONV_K, CONV_WIDTH), CONV_K ** -0.5),
        'lower_bounds': nrm(ks[4], (DEPTH, HGRN_WIDTH), 0.1),
        'head_norm_g': 1.0 + nrm(ks[5], (DEPTH, D_MIX), 0.02),
        'w_out': nrm(ks[6], (DEPTH, D_MIX, D_MODEL), D_MIX ** -0.5 * DEEPNORM_BETA),
        'ln1_g': 1.0 + nrm(ks[7], (DEPTH, D_MODEL), 0.02),
        'ln1_b': nrm(ks[8], (DEPTH, D_MODEL), 0.02),
        'w_router': nrm(ks[9], (DEPTH, D_MODEL, N_EXPERTS), D_MODEL ** -0.5),
        'b_router': nrm(ks[10], (DEPTH, N_EXPERTS), 0.01),
        'w1': nrm(ks[11], (DEPTH, N_EXPERTS, D_MODEL, 2 * D_FF), D_MODEL ** -0.5 * DEEPNORM_BETA),
        'b1': nrm(ks[12], (DEPTH, N_EXPERTS, 2 * D_FF), 0.02),
        'w2': nrm(ks[13], (DEPTH, N_EXPERTS, D_FF, D_MODEL), D_FF ** -0.5 * DEEPNORM_BETA),
        'b2': nrm(ks[14], (DEPTH, N_EXPERTS, D_MODEL), 0.02),
        'ln2_g': 1.0 + nrm(ks[15], (DEPTH, D_MODEL), 0.02),
        'ln2_b': nrm(ks[16], (DEPTH, D_MODEL), 0.02),
    }


def reference(x, w_in, b_fgate, conv_w, lower_bounds, head_norm_g, w_out, ln1_g, ln1_b,
              w_router, b_router, w1, b1, w2, b2, ln2_g, ln2_b):
    lb = jnp.cumsum(jax.nn.softmax(lower_bounds.astype(jnp.float32), axis=0), axis=0)
    lb = lb - lb[0]
    for l in range(DEPTH):
        mix = hybrid_mixer(x, w_in[l], b_fgate[l], conv_w[l], lb[l], head_norm_g[l], w_out[l])
        x = layer_norm(DEEPNORM_ALPHA * x + mix, ln1_g[l], ln1_b[l])
        ffn = moe_ffn(x, w_router[l], b_router[l], w1[l], b1[l], w2[l], b2[l])
        x = layer_norm(DEEPNORM_ALPHA * x + ffn, ln2_g[l], ln2_b[l])
    return x
```

```python
import functools

import jax
import jax.numpy as jnp
import numpy as np
from jax import lax
from jax.experimental import pallas as pl
from jax.experimental.pallas import tpu as pltpu

F32 = jnp.float32
BF16 = jnp.bfloat16
I32 = jnp.int32
U32 = jnp.uint32

HEAD_DIM = 128
ATT_HEADS = 8
CONV_K = 3
HGRN_HEADS = 4
HGRN_CHUNK = 16
N_EXPERTS = 32
TOP_K = 4
SWIGLU_LIMIT = 7.0
SWIGLU_ALPHA = 1.702
LN_EPS = 1e-5
RMS_EPS = 1e-6
NEG_INF = float("-inf")

V7X_VMEM_BYTES = 64 * 1024 * 1024
VMEM_LIMIT = 56 * 1024 * 1024

PROJ_COLS = 7168
COL_F = 36
COL512_CONV_B, COL512_CONV_C, COL512_CONV_H = 6, 7, 8
COL512_HQ, COL512_HF, COL512_HI, COL512_HG = 10, 11, 12, 13


def _cparams(sem):
    return pltpu.CompilerParams(dimension_semantics=sem, vmem_limit_bytes=VMEM_LIMIT)


def _log_sigmoid(z):
    return jnp.minimum(z, 0.0) - jnp.log1p(jnp.exp(-jnp.abs(z)))


def _mm_kernel(x_ref, w_ref, o_ref, wb_ref):
    @pl.when(pl.program_id(1) == 0)
    def _():
        wb_ref[...] = w_ref[...].astype(BF16)

    o_ref[...] = jnp.dot(x_ref[...].astype(BF16), wb_ref[...],
                         preferred_element_type=F32).astype(o_ref.dtype)


def _matmul(x, w, tm, tn, out_dtype, name):
    m, k = x.shape
    n = w.shape[1]
    return pl.pallas_call(
        _mm_kernel,
        grid=(n // tn, m // tm),
        in_specs=[pl.BlockSpec((tm, k), lambda j, i: (i, 0)),
                  pl.BlockSpec((k, tn), lambda j, i: (0, j))],
        out_specs=pl.BlockSpec((tm, tn), lambda j, i: (i, j)),
        out_shape=jax.ShapeDtypeStruct((m, n), out_dtype),
        scratch_shapes=[pltpu.VMEM((k, tn), BF16)],
        compiler_params=_cparams(("arbitrary", "arbitrary")),
        name=name,
    )(x, w)


def _fcum_kernel(f_ref, b_ref, ccol_ref, crow_ref, *, seq, blk):
    ls = _log_sigmoid(f_ref[...] + b_ref[...])
    r = lax.broadcasted_iota(I32, (blk, blk), 0)
    c = lax.broadcasted_iota(I32, (blk, blk), 1)
    tri = (r >= c).astype(F32)
    carry = jnp.zeros((1, 128), F32)
    for i in range(seq // blk):
        cs = jnp.dot(tri, ls[i * blk:(i + 1) * blk], precision=lax.Precision.HIGHEST,
                     preferred_element_type=F32) + carry
        ccol_ref[i * blk:(i + 1) * blk, :] = cs
        carry = cs[blk - 1:blk, :]
    crow_ref[...] = ccol_ref[...].T


def _forget_cumsum(proj, b_fgate_pad, batch, seq):
    t = batch * seq
    return pl.pallas_call(
        functools.partial(_fcum_kernel, seq=seq, blk=256),
        grid=(batch,),
        in_specs=[pl.BlockSpec((seq, 128), lambda b: (b, COL_F)),
                  pl.BlockSpec((1, 128), lambda b: (0, 0))],
        out_specs=[pl.BlockSpec((seq, 128), lambda b: (b, 0)),
                   pl.BlockSpec((None, 128, seq), lambda b: (b, 0, 0))],
        out_shape=[jax.ShapeDtypeStruct((t, 128), F32),
                   jax.ShapeDtypeStruct((batch, 128, seq), F32)],
        compiler_params=_cparams(("arbitrary",)),
        name="forget_cumsum",
    )(proj, b_fgate_pad)


def _att_kernel(q_ref, k_ref, v_ref, cc_ref, cr_ref, o_ref, m_ref, l_ref, acc_ref, *, tq, tk, nk):
    i = pl.program_id(1)
    j = pl.program_id(2)
    scale = HEAD_DIM ** -0.5

    @pl.when(j == 0)
    def _():
        m_ref[...] = jnp.full(m_ref.shape, NEG_INF, F32)
        l_ref[...] = jnp.zeros(l_ref.shape, F32)
        acc_ref[...] = jnp.zeros(acc_ref.shape, F32)

    @pl.when(j <= i)
    def _():
        q_pos = i * tq + lax.broadcasted_iota(I32, (tq, tk), 0)
        k_pos = j * tk + lax.broadcasted_iota(I32, (tq, tk), 1)
        mask = k_pos <= q_pos
        for h in range(ATT_HEADS):
            hs = slice(h * HEAD_DIM, (h + 1) * HEAD_DIM)
            q = q_ref[:, hs].astype(BF16)
            k = k_ref[:, hs].astype(BF16)
            v = v_ref[:, hs].astype(BF16)
            s = lax.dot_general(q, k, (((1,), (1,)), ((), ())), preferred_element_type=F32)
            s = s * scale + (cc_ref[:, h:h + 1] - cr_ref[h:h + 1, :])
            s = jnp.where(mask, s, NEG_INF)
            m_prev = m_ref[h]
            m_new = jnp.maximum(m_prev, jnp.max(s, axis=-1, keepdims=True))
            p = jnp.exp(s - m_new)
            alpha = jnp.exp(m_prev - m_new)
            l_ref[h] = alpha * l_ref[h] + jnp.sum(p, axis=-1, keepdims=True)
            acc_ref[:, hs] = alpha * acc_ref[:, hs] + jnp.dot(p.astype(BF16), v,
                                                            preferred_element_type=F32)
            m_ref[h] = m_new

    @pl.when(j == nk - 1)
    def _():
        for h in range(ATT_HEADS):
            hs = slice(h * HEAD_DIM, (h + 1) * HEAD_DIM)
            o_ref[:, hs] = acc_ref[:, hs] / l_ref[h]


def _attention(proj, ccol, crow, batch, seq, tq=512):
    t = batch * seq
    nq = seq // tq
    width = ATT_HEADS * HEAD_DIM
    kernel = functools.partial(_att_kernel, tq=tq, tk=tq, nk=nq)
    return pl.pallas_call(
        kernel,
        grid=(batch, nq, nq),
        in_specs=[pl.BlockSpec((tq, width), lambda b, i, j: (b * nq + i, 0)),
                  pl.BlockSpec((tq, width), lambda b, i, j: (b * nq + jnp.minimum(j, i), 1)),
                  pl.BlockSpec((tq, width), lambda b, i, j: (b * nq + jnp.minimum(j, i), 2)),
                  pl.BlockSpec((tq, 128), lambda b, i, j: (b * nq + i, 0)),
                  pl.BlockSpec((None, 8, tq), lambda b, i, j: (b, 0, jnp.minimum(j, i)))],
        out_specs=pl.BlockSpec((tq, width), lambda b, i, j: (b * nq + i, 0)),
        out_shape=jax.ShapeDtypeStruct((t, width), F32),
        scratch_shapes=[pltpu.VMEM((ATT_HEADS, tq, 1), F32),
                        pltpu.VMEM((ATT_HEADS, tq, 1), F32),
                        pltpu.VMEM((tq, width), F32)],
        compiler_params=_cparams(("arbitrary", "arbitrary", "arbitrary")),
        name="fox_attention",
    )(proj, proj, proj, ccol, crow)


def _conv_kernel(b_ref, c_ref, h_ref, w_ref, o_ref, carry_ref, *, ts):
    @pl.when(pl.program_id(1) == 0)
    def _():
        carry_ref[...] = jnp.zeros(carry_ref.shape, F32)

    u = c_ref[...] * h_ref[...]
    ext = jnp.concatenate([carry_ref[...], u], axis=0)
    u1 = ext[7:7 + ts]
    u2 = ext[6:6 + ts]
    w = w_ref[...]
    y = w[2:3] * u + w[1:2] * u1 + w[0:1] * u2
    o_ref[...] = b_ref[...] * y
    carry_ref[...] = u[ts - 8:ts]


def _short_conv(proj, conv_w, batch, seq, ts=512):
    t = batch * seq
    ns = seq // ts
    w = conv_w.shape[1]
    return pl.pallas_call(
        functools.partial(_conv_kernel, ts=ts),
        grid=(batch, ns),
        in_specs=[pl.BlockSpec((ts, w), lambda b, i: (b * ns + i, COL512_CONV_B)),
                  pl.BlockSpec((ts, w), lambda b, i: (b * ns + i, COL512_CONV_C)),
                  pl.BlockSpec((ts, w), lambda b, i: (b * ns + i, COL512_CONV_H)),
                  pl.BlockSpec((CONV_K, w), lambda b, i: (0, 0))],
        out_specs=pl.BlockSpec((ts, w), lambda b, i: (b * ns + i, 0)),
        out_shape=jax.ShapeDtypeStruct((t, w), F32),
        scratch_shapes=[pltpu.VMEM((8, w), F32)],
        compiler_params=_cparams(("arbitrary", "arbitrary")),
        name="short_conv",
    )(proj, proj, proj, conv_w)


def _hgrn_kernel(q_ref, f_ref, i_ref, lb_ref, o_ref, st_ref, qs_ref, kk_ref, bc_ref, *, bs):
    c = HGRN_CHUNK

    @pl.when(pl.program_id(1) == 0)
    def _():
        st_ref[...] = jnp.zeros(st_ref.shape, F32)

    z = f_ref[...]
    lb = lb_ref[...]
    ls = _log_sigmoid(z)
    a = jnp.log(lb)
    b = jnp.log1p(-lb) + ls
    log_f = jnp.maximum(a, b) + jnp.log1p(jnp.exp(-jnp.abs(a - b)))
    kk_ref[...] = (1.0 - lb) * jnp.exp(ls - z)
    q = q_ref[...]
    qs_ref[...] = q / (1.0 + jnp.exp(-q))
    r = lax.broadcasted_iota(I32, (bs, bs), 0)
    cc = lax.broadcasted_iota(I32, (bs, bs), 1)
    tri = ((r // c == cc // c) & (r >= cc)).astype(F32)
    bc_ref[...] = jnp.dot(tri, log_f, precision=lax.Precision.HIGHEST, preferred_element_type=F32)

    t_idx = lax.broadcasted_iota(I32, (c, HEAD_DIM), 0)
    ones = jnp.ones((HEAD_DIM, HEAD_DIM), BF16)

    def chunk(n, carry):
        rows = pl.ds(pl.multiple_of(n * c, c), c)
        for h in range(HGRN_HEADS):
            hs = slice(h * HEAD_DIM, (h + 1) * HEAD_DIM)
            qh = qs_ref[rows, hs]
            kh = kk_ref[rows, hs]
            vh = i_ref[rows, hs]
            bh = bc_ref[rows, hs]
            b_last = bh[c - 1:c]
            parts = []
            for s in range(c):
                d = jnp.where(t_idx >= s, bh - bh[s:s + 1], NEG_INF)
                parts.append(qh * jnp.exp(d) * kh[s:s + 1])
            a3 = jnp.concatenate(parts, axis=0).astype(BF16)
            rs = jnp.dot(a3, ones, preferred_element_type=F32)
            o = lax.dot_general((qh * jnp.exp(bh)).astype(BF16), st_ref[h].astype(BF16),
                                (((1,), (1,)), ((), ())), preferred_element_type=F32)
            for s in range(c):
                o = o + rs[s * c:(s + 1) * c] * vh[s:s + 1]
            o_ref[rows, hs] = o
            kd = kh * jnp.exp(b_last - bh)
            kv_t = lax.dot_general(vh.astype(BF16), kd.astype(BF16), (((0,), (0,)), ((), ())),
                                   preferred_element_type=F32)
            st_ref[h] = jnp.exp(b_last) * st_ref[h] + kv_t
        return carry

    lax.fori_loop(0, bs // c, chunk, 0)


def _hgrn2(proj, lb, batch, seq, bs=256):
    t = batch * seq
    ns = seq // bs
    w = HGRN_HEADS * HEAD_DIM
    return pl.pallas_call(
        functools.partial(_hgrn_kernel, bs=bs),
        grid=(batch, ns),
        in_specs=[pl.BlockSpec((bs, w), lambda b, i: (b * ns + i, COL512_HQ)),
                  pl.BlockSpec((bs, w), lambda b, i: (b * ns + i, COL512_HF)),
                  pl.BlockSpec((bs, w), lambda b, i: (b * ns + i, COL512_HI)),
                  pl.BlockSpec((1, w), lambda b, i: (0, 0))],
        out_specs=pl.BlockSpec((bs, w), lambda b, i: (b * ns + i, 0)),
        out_shape=jax.ShapeDtypeStruct((t, w), F32),
        scratch_shapes=[pltpu.VMEM((HGRN_HEADS, HEAD_DIM, HEAD_DIM), F32),
                        pltpu.VMEM((bs, w), F32),
                        pltpu.VMEM((bs, w), F32),
                        pltpu.VMEM((bs, w), F32)],
        compiler_params=_cparams(("arbitrary", "arbitrary")),
        name="hgrn2",
    )(proj, proj, proj, lb)


def _headnorm_kernel(a_ref, c_ref, r_ref, hg_ref, g_ref, o_ref):
    na = a_ref.shape[1] // HEAD_DIM
    nc = c_ref.shape[1] // HEAD_DIM
    nr = r_ref.shape[1] // HEAD_DIM

    def rms(x, g):
        ms = jnp.mean(x * x, axis=-1, keepdims=True)
        return x * lax.rsqrt(ms + RMS_EPS) * g

    col = 0
    for src, n, gated in ((a_ref, na, False), (c_ref, nc, False), (r_ref, nr, True)):
        for h in range(n):
            hs = slice(h * HEAD_DIM, (h + 1) * HEAD_DIM)
            os_ = slice(col, col + HEAD_DIM)
            y = rms(src[:, hs], g_ref[:, os_])
            if gated:
                hg = hg_ref[:, hs]
                y = y * (hg / (1.0 + jnp.exp(-hg)))
            o_ref[:, os_] = y.astype(o_ref.dtype)
            col += HEAD_DIM


def _head_norm(o_att, o_conv, o_rec, proj, g, tm=256):
    t = o_att.shape[0]
    wa, wc, wr = o_att.shape[1], o_conv.shape[1], o_rec.shape[1]
    d = wa + wc + wr
    return pl.pallas_call(
        _headnorm_kernel,
        grid=(t // tm,),
        in_specs=[pl.BlockSpec((tm, wa), lambda i: (i, 0)),
                  pl.BlockSpec((tm, wc), lambda i: (i, 0)),
                  pl.BlockSpec((tm, wr), lambda i: (i, 0)),
                  pl.BlockSpec((tm, wr), lambda i: (i, COL512_HG)),
                  pl.BlockSpec((1, d), lambda i: (0, 0))],
        out_specs=pl.BlockSpec((tm, d), lambda i: (i, 0)),
        out_shape=jax.ShapeDtypeStruct((t, d), BF16),
        compiler_params=_cparams(("arbitrary",)),
        name="head_norm",
    )(o_att, o_conv, o_rec, proj, g)


def _pack_bf16_pairs(y):
    half = y.shape[1] // 2
    lo = pltpu.bitcast(y[:, :half].astype(BF16).astype(F32), U32) >> 16
    hi = pltpu.bitcast(y[:, half:].astype(BF16).astype(F32), U32) & jnp.uint32(0xFFFF0000)
    return lo | hi


def _layer_norm_rows(z, g, b):
    mu = jnp.mean(z, axis=-1, keepdims=True)
    zc = z - mu
    var = jnp.mean(zc * zc, axis=-1, keepdims=True)
    return zc * lax.rsqrt(var + LN_EPS) * g + b


def _addln_kernel(x_ref, y_ref, g_ref, b_ref, o_ref, ob_ref, op_ref, *, alpha):
    y = _layer_norm_rows(alpha * x_ref[...] + y_ref[...], g_ref[...], b_ref[...])
    o_ref[...] = y
    ob_ref[...] = y.astype(BF16)
    op_ref[...] = _pack_bf16_pairs(y)


def _add_ln(x, y, g, b, alpha, tm=256):
    t, d = x.shape
    row = pl.BlockSpec((tm, d), lambda i: (i, 0))
    vec = pl.BlockSpec((1, d), lambda i: (0, 0))
    return pl.pallas_call(
        functools.partial(_addln_kernel, alpha=alpha),
        grid=(t // tm,),
        in_specs=[row, row, vec, vec],
        out_specs=[row, row, pl.BlockSpec((tm, d // 2), lambda i: (i, 0))],
        out_shape=[jax.ShapeDtypeStruct((t, d), F32),
                   jax.ShapeDtypeStruct((t, d), BF16),
                   jax.ShapeDtypeStruct((t, d // 2), U32)],
        compiler_params=_cparams(("arbitrary",)),
        name="add_layernorm",
    )(x, y, g, b)


def _router_kernel(x_ref, w_ref, b_ref, idx_ref, wt_ref):
    logits = jnp.dot(x_ref[...], w_ref[...], precision=lax.Precision.HIGHEST,
                     preferred_element_type=F32) + b_ref[...]
    lane = lax.broadcasted_iota(I32, logits.shape, 1)
    logits = jnp.where(lane < N_EXPERTS, logits, NEG_INF)
    vals, idxs = [], []
    for _ in range(TOP_K):
        m = jnp.max(logits, axis=-1, keepdims=True)
        sel = jnp.min(jnp.where(logits == m, lane, 128), axis=-1, keepdims=True)
        vals.append(m)
        idxs.append(sel)
        logits = jnp.where(lane == sel, NEG_INF, logits)
    es = [jnp.exp(v - vals[0]) for v in vals]
    tot = es[0] + es[1] + es[2] + es[3]
    idx_out = jnp.zeros(logits.shape, I32)
    wt_out = jnp.zeros(logits.shape, F32)
    for k in range(TOP_K):
        idx_out = jnp.where(lane == k, idxs[k], idx_out)
        wt_out = jnp.where(lane == k, es[k] / tot, wt_out)
    idx_ref[...] = idx_out
    wt_ref[...] = wt_out


def _router(x, w_pad, b_pad, tm=512):
    t, d = x.shape
    return pl.pallas_call(
        _router_kernel,
        grid=(t // tm,),
        in_specs=[pl.BlockSpec((tm, d), lambda i: (i, 0)),
                  pl.BlockSpec((d, 128), lambda i: (0, 0)),
                  pl.BlockSpec((1, 128), lambda i: (0, 0))],
        out_specs=[pl.BlockSpec((tm, 128), lambda i: (i, 0)),
                   pl.BlockSpec((tm, 128), lambda i: (i, 0))],
        out_shape=[jax.ShapeDtypeStruct((t, 128), I32),
                   jax.ShapeDtypeStruct((t, 128), F32)],
        compiler_params=_cparams(("arbitrary",)),
        name="router_top4",
    )(x, w_pad, b_pad)


MOE_TM = 256
MOE_TN = 256
MOE_CAP = 1536


def _moe_plan(top_idx, n_tokens):
    n_pairs = n_tokens * TOP_K
    g_max = N_EXPERTS + -(-n_pairs // MOE_CAP)
    e_flat = top_idx.reshape(-1)
    onehot = (e_flat[:, None] == jnp.arange(N_EXPERTS, dtype=I32)[None, :]).astype(I32)
    csum = jnp.cumsum(onehot, axis=0)
    rank = jnp.sum(csum * onehot, axis=1) - 1
    counts = csum[-1]
    n_grp = (counts + MOE_CAP - 1) // MOE_CAP
    g_end = jnp.cumsum(n_grp)
    g_base = g_end - n_grp
    pair = jnp.arange(n_pairs, dtype=I32)
    dst = (pair % TOP_K) * n_tokens + pair // TOP_K
    grp = g_base[e_flat] + rank // MOE_CAP
    row = rank % MOE_CAP
    trash = TOP_K * n_tokens + (jnp.arange(MOE_CAP, dtype=I32) % MOE_TM)
    table = jnp.broadcast_to(trash[None, :], (g_max, MOE_CAP))
    table = table.at[grp, row].set(dst)
    gid = jnp.arange(g_max, dtype=I32)
    n_used = g_end[-1]
    g_exp = jnp.searchsorted(g_end, jnp.minimum(gid, n_used - 1), side="right").astype(I32)
    g_rows = jnp.clip(counts[g_exp] - (gid - g_base[g_exp]) * MOE_CAP, 0, MOE_CAP)
    g_rows = jnp.where(gid < n_used, g_rows, 0)
    g_tiles = ((g_rows + MOE_TM - 1) // MOE_TM).astype(I32)
    return table, g_exp, g_tiles, g_max


def _moe_kernel(ge_ref, gt_ref, tbl_hbm, x_hbm, w1g_ref, w1u_ref, b1g_ref, b1u_ref, w2_ref, b2_ref,
                out_hbm, tbl_smem, xbuf, yacc, w1g_b, w1u_b, w2_b, sem_tbl, sem_in, sem_out,
                *, n_tokens, n_steps):
    g = pl.program_id(0)
    j = pl.program_id(1)
    nt = gt_ref[g]
    tm = MOE_TM
    half = xbuf.shape[1]

    def row_copy_in(r):
        tok = tbl_smem[r] & (n_tokens - 1)
        return pltpu.make_async_copy(x_hbm.at[pl.ds(tok, 1)], xbuf.at[pl.ds(r, 1)], sem_in)

    def row_copy_out(r):
        return pltpu.make_async_copy(yacc.at[pl.ds(r, 1)], out_hbm.at[pl.ds(tbl_smem[r], 1)], sem_out)

    @pl.when((g == 0) & (j == 0))
    def _():
        yacc[pl.ds(0, tm), :] = jnp.zeros((tm, yacc.shape[1]), F32)
        cp = pltpu.make_async_copy(yacc.at[pl.ds(0, tm)], out_hbm.at[pl.ds(TOP_K * n_tokens, tm)], sem_out)
        cp.start()
        cp.wait()

    @pl.when(nt > 0)
    def _():
        @pl.when(j == 0)
        def _():
            cp = pltpu.make_async_copy(tbl_hbm.at[g], tbl_smem, sem_tbl)
            cp.start()
            cp.wait()

            def issue(r, c):
                row_copy_in(r).start()
                return c

            lax.fori_loop(0, nt * tm, issue, 0)

            def wait_tile(m, c):
                pltpu.make_async_copy(x_hbm.at[pl.ds(0, tm)], xbuf.at[pl.ds(0, tm)], sem_in).wait()
                return c

            lax.fori_loop(0, nt, wait_tile, 0)

        w1g_b[...] = w1g_ref[...].astype(BF16)
        w1u_b[...] = w1u_ref[...].astype(BF16)
        w2_b[...] = w2_ref[...].astype(BF16)

        def tile(m, c):
            rows = pl.ds(pl.multiple_of(m * tm, tm), tm)
            xw = xbuf[rows, :]
            xlo = pltpu.bitcast(xw << 16, F32).astype(BF16)
            xhi = pltpu.bitcast(xw & jnp.uint32(0xFFFF0000), F32).astype(BF16)
            hg = (jnp.dot(xlo, w1g_b[:half], preferred_element_type=F32)
                  + jnp.dot(xhi, w1g_b[half:], preferred_element_type=F32) + b1g_ref[...])
            hu = (jnp.dot(xlo, w1u_b[:half], preferred_element_type=F32)
                  + jnp.dot(xhi, w1u_b[half:], preferred_element_type=F32) + b1u_ref[...])
            gate = jnp.minimum(hg, SWIGLU_LIMIT)
            up = jnp.clip(hu, -SWIGLU_LIMIT, SWIGLU_LIMIT)
            act = (up + 1.0) * gate / (1.0 + jnp.exp(-SWIGLU_ALPHA * gate))
            y = jnp.dot(act.astype(BF16), w2_b[...], preferred_element_type=F32)

            @pl.when(j == 0)
            def _():
                yacc[rows, :] = y + b2_ref[...]

            @pl.when(j > 0)
            def _():
                yacc[rows, :] += y

            return c

        lax.fori_loop(0, nt, tile, 0)

        @pl.when(j == n_steps - 1)
        def _():
            def issue(r, c):
                row_copy_out(r).start()
                return c

            lax.fori_loop(0, nt * tm, issue, 0)

            def wait_tile(m, c):
                pltpu.make_async_copy(yacc.at[pl.ds(0, tm)], out_hbm.at[pl.ds(0, tm)], sem_out).wait()
                return c

            lax.fori_loop(0, nt, wait_tile, 0)


def _moe_ffn(xp, table, g_exp, g_tiles, g_max, w1, b1, w2, b2, layer, n_tokens):
    half = xp.shape[1]
    d = 2 * half
    d_ff = w2.shape[2]
    n_steps = d_ff // MOE_TN
    tn = MOE_TN

    def jj(g, j, gt):
        return jnp.where(gt[g] > 0, j, n_steps - 1)

    grid_spec = pltpu.PrefetchScalarGridSpec(
        num_scalar_prefetch=2,
        grid=(g_max, n_steps),
        in_specs=[
            pl.BlockSpec(memory_space=pl.ANY),
            pl.BlockSpec(memory_space=pl.ANY),
            pl.BlockSpec((None, None, d, tn), lambda g, j, ge, gt: (layer, ge[g], 0, jj(g, j, gt))),
            pl.BlockSpec((None, None, d, tn), lambda g, j, ge, gt: (layer, ge[g], 0, jj(g, j, gt) + n_steps)),
            pl.BlockSpec((None, None, 1, tn), lambda g, j, ge, gt: (layer, ge[g], 0, jj(g, j, gt))),
            pl.BlockSpec((None, None, 1, tn), lambda g, j, ge, gt: (layer, ge[g], 0, jj(g, j, gt) + n_steps)),
            pl.BlockSpec((None, None, tn, d), lambda g, j, ge, gt: (layer, ge[g], jj(g, j, gt), 0)),
            pl.BlockSpec((None, None, 1, d), lambda g, j, ge, gt: (layer, ge[g], 0, 0)),
        ],
        out_specs=pl.BlockSpec(memory_space=pl.ANY),
        scratch_shapes=[
            pltpu.SMEM((MOE_CAP,), I32),
            pltpu.VMEM((MOE_CAP, half), U32),
            pltpu.VMEM((MOE_CAP, d), F32),
            pltpu.VMEM((d, tn), BF16),
            pltpu.VMEM((d, tn), BF16),
            pltpu.VMEM((tn, d), BF16),
            pltpu.SemaphoreType.DMA,
            pltpu.SemaphoreType.DMA,
            pltpu.SemaphoreType.DMA,
        ],
    )
    return pl.pallas_call(
        functools.partial(_moe_kernel, n_tokens=n_tokens, n_steps=n_steps),
        grid_spec=grid_spec,
        out_shape=jax.ShapeDtypeStruct((TOP_K * n_tokens + MOE_TM, d), F32),
        compiler_params=_cparams(("arbitrary", "arbitrary")),
        name="moe_ffn",
    )(g_exp, g_tiles, table, xp, w1, w1, b1, b1, w2, b2)


def _combine_kernel(y0_ref, y1_ref, y2_ref, y3_ref, wt_ref, x_ref, g_ref, b_ref, o_ref, ob_ref, *, alpha):
    wt = wt_ref[...]
    y = (wt[:, 0:1] * y0_ref[...] + wt[:, 1:2] * y1_ref[...]
         + wt[:, 2:3] * y2_ref[...] + wt[:, 3:4] * y3_ref[...])
    z = _layer_norm_rows(alpha * x_ref[...] + y, g_ref[...], b_ref[...])
    o_ref[...] = z
    ob_ref[...] = z.astype(BF16)


def _combine_ln(y4, wts, x, g, b, alpha, tm=256):
    t, d = x.shape
    nb = t // tm
    row = pl.BlockSpec((tm, d), lambda i: (i, 0))
    vec = pl.BlockSpec((1, d), lambda i: (0, 0))
    slot = [pl.BlockSpec((tm, d), functools.partial(lambda i, k: (k * nb + i, 0), k=k)) for k in range(TOP_K)]
    return pl.pallas_call(
        functools.partial(_combine_kernel, alpha=alpha),
        grid=(nb,),
        in_specs=slot + [pl.BlockSpec((tm, 128), lambda i: (i, 0)), row, vec, vec],
        out_specs=[row, row],
        out_shape=[jax.ShapeDtypeStruct((t, d), F32), jax.ShapeDtypeStruct((t, d), BF16)],
        compiler_params=_cparams(("arbitrary",)),
        name="combine_layernorm",
    )(y4, y4, y4, y4, wts, x, g, b)


def _pack_w_in(w_in_l):
    d = w_in_l.shape[0]
    att = ATT_HEADS * HEAD_DIM * 3
    n_f = ATT_HEADS
    conv = 3 * 4 * HEAD_DIM
    return jnp.concatenate([
        w_in_l[:, :att],
        w_in_l[:, att + n_f:att + n_f + conv],
        w_in_l[:, att:att + n_f],
        jnp.zeros((d, 512 - n_f), F32),
        w_in_l[:, att + n_f + conv:],
    ], axis=1)


def kernel(x, w_in, b_fgate, conv_w, lower_bounds, head_norm_g, w_out, ln1_g, ln1_b, w_router, b_router,
           w1, b1, w2, b2, ln2_g, ln2_b):
    batch, seq, d = x.shape
    depth = w_in.shape[0]
    t = batch * seq
    alpha = (2 * depth) ** 0.25

    lb = jnp.cumsum(jax.nn.softmax(lower_bounds.astype(F32), axis=0), axis=0)
    lb = lb - lb[0]
    b1r = b1.reshape(depth, N_EXPERTS, 1, b1.shape[-1])
    b2r = b2.reshape(depth, N_EXPERTS, 1, b2.shape[-1])

    xf = x.reshape(t, d)
    xb = xf.astype(BF16)
    for l in range(depth):
        proj = _matmul(xb, _pack_w_in(w_in[l]), 512, 1024, F32, "in_proj")
        bf_pad = jnp.zeros((1, 128), F32).at[0, :ATT_HEADS].set(b_fgate[l])
        ccol, crow = _forget_cumsum(proj, bf_pad, batch, seq)
        o_att = _attention(proj, ccol, crow, batch, seq)
        o_conv = _short_conv(proj, conv_w[l], batch, seq)
        o_rec = _hgrn2(proj, lb[l][None, :], batch, seq)
        o = _head_norm(o_att, o_conv, o_rec, proj, head_norm_g[l][None, :])
        mix = _matmul(o, w_out[l], 512, 1024, F32, "out_proj")
        x1, _, x1p = _add_ln(xf, mix, ln1_g[l][None, :], ln1_b[l][None, :], alpha)

        wr_pad = jnp.zeros((d, 128), F32).at[:, :N_EXPERTS].set(w_router[l])
        br_pad = jnp.zeros((1, 128), F32).at[0, :N_EXPERTS].set(b_router[l])
        idx, wts = _router(x1, wr_pad, br_pad)
        table, g_exp, g_tiles, g_max = _moe_plan(idx[:, :TOP_K], t)
        y4 = _moe_ffn(x1p, table, g_exp, g_tiles, g_max, w1, b1r, w2, b2r, l, t)
        xf, xb = _combine_ln(y4, wts, x1, ln2_g[l][None, :], ln2_b[l][None, :], alpha)
    return xf.reshape(batch, seq, d)
```

```python
import functools

import jax
import jax.numpy as jnp
import numpy as np
from jax import lax
from jax.experimental import pallas as pl
from jax.experimental.pallas import tpu as pltpu

F32 = jnp.float32
BF16 = jnp.bfloat16
I32 = jnp.int32
U32 = jnp.uint32

HEAD_DIM = 128
ATT_HEADS = 8
CONV_K = 3
HGRN_HEADS = 4
HGRN_CHUNK = 16
N_EXPERTS = 32
TOP_K = 4
SWIGLU_LIMIT = 7.0
SWIGLU_ALPHA = 1.702
LN_EPS = 1e-5
RMS_EPS = 1e-6
NEG_INF = float("-inf")
LOG2_E = 1.4426950408889634

V7X_VMEM_BYTES = 64 * 1024 * 1024
VMEM_LIMIT = 56 * 1024 * 1024

PROJ_COLS = 7168
COL_F = 36
COL512_CONV_B, COL512_CONV_C, COL512_CONV_H = 6, 7, 8
COL512_HQ, COL512_HF, COL512_HI, COL512_HG = 10, 11, 12, 13


def _cparams(sem):
    return pltpu.CompilerParams(dimension_semantics=sem, vmem_limit_bytes=VMEM_LIMIT)


def _log_sigmoid(z):
    return jnp.minimum(z, 0.0) - jnp.log1p(jnp.exp(-jnp.abs(z)))


def _mm_kernel(x_ref, w_ref, o_ref, wb_ref):
    @pl.when(pl.program_id(1) == 0)
    def _():
        wb_ref[...] = w_ref[...].astype(BF16)

    o_ref[...] = jnp.dot(x_ref[...].astype(BF16), wb_ref[...],
                         preferred_element_type=F32).astype(o_ref.dtype)


def _matmul(x, w, tm, tn, out_dtype, name):
    m, k = x.shape
    n = w.shape[1]
    return pl.pallas_call(
        _mm_kernel,
        grid=(n // tn, m // tm),
        in_specs=[pl.BlockSpec((tm, k), lambda j, i: (i, 0)),
                  pl.BlockSpec((k, tn), lambda j, i: (0, j))],
        out_specs=pl.BlockSpec((tm, tn), lambda j, i: (i, j)),
        out_shape=jax.ShapeDtypeStruct((m, n), out_dtype),
        scratch_shapes=[pltpu.VMEM((k, tn), BF16)],
        compiler_params=_cparams(("arbitrary", "arbitrary")),
        name=name,
    )(x, w)


def _fcum_kernel(f_ref, b_ref, ccol_ref, *, seq, blk):
    ls = _log_sigmoid(f_ref[...] + b_ref[...])
    r = lax.broadcasted_iota(I32, (blk, blk), 0)
    c = lax.broadcasted_iota(I32, (blk, blk), 1)
    tri = (r >= c).astype(F32)
    carry = jnp.zeros((1, 128), F32)
    for i in range(seq // blk):
        cs = jnp.dot(tri, ls[i * blk:(i + 1) * blk], precision=lax.Precision.HIGHEST,
                     preferred_element_type=F32) + carry
        ccol_ref[i * blk:(i + 1) * blk, :] = cs
        carry = cs[blk - 1:blk, :]


def _forget_cumsum(proj, b_fgate_pad, batch, seq):
    t = batch * seq
    return pl.pallas_call(
        functools.partial(_fcum_kernel, seq=seq, blk=256),
        grid=(batch,),
        in_specs=[pl.BlockSpec((seq, 128), lambda b: (b, COL_F)),
                  pl.BlockSpec((1, 128), lambda b: (0, 0))],
        out_specs=pl.BlockSpec((seq, 128), lambda b: (b, 0)),
        out_shape=jax.ShapeDtypeStruct((t, 128), F32),
        compiler_params=_cparams(("arbitrary",)),
        name="forget_cumsum",
    )(proj, b_fgate_pad)


ATT_AUG = 2 * HEAD_DIM


def _split3_bf16(c):
    hi = c.astype(BF16).astype(F32)
    r1 = c - hi
    mid = r1.astype(BF16).astype(F32)
    lo = (r1 - mid).astype(BF16).astype(F32)
    return hi, mid, lo


def _attprep_kernel(q_ref, k_ref, v_ref, cc_ref, qa_ref, ka_ref, vb_ref):
    scale = HEAD_DIM ** -0.5 * LOG2_E
    tm = q_ref.shape[0]
    lane = lax.broadcasted_iota(I32, (tm, HEAD_DIM), 1)
    vb_ref[...] = v_ref[...].T.astype(BF16)
    for h in range(ATT_HEADS):
        hs = slice(h * HEAD_DIM, (h + 1) * HEAD_DIM)
        c = jnp.broadcast_to(cc_ref[:, h:h + 1] * LOG2_E, (tm, HEAD_DIM))
        hi, mid, lo = _split3_bf16(c)
        one = jnp.ones_like(c)
        zero = jnp.zeros_like(c)
        q_aug = jnp.where(lane == 0, hi, jnp.where(lane == 1, mid, jnp.where(lane == 2, lo,
                          jnp.where(lane < 6, one, zero))))
        k_aug = jnp.where(lane < 3, one, jnp.where(lane == 3, -hi, jnp.where(lane == 4, -mid,
                          jnp.where(lane == 5, -lo, zero))))
        qa_ref[:, h * ATT_AUG:h * ATT_AUG + HEAD_DIM] = (q_ref[:, hs] * scale).astype(BF16)
        qa_ref[:, h * ATT_AUG + HEAD_DIM:(h + 1) * ATT_AUG] = q_aug.astype(BF16)
        ka_ref[:, h * ATT_AUG:h * ATT_AUG + HEAD_DIM] = k_ref[:, hs].astype(BF16)
        ka_ref[:, h * ATT_AUG + HEAD_DIM:(h + 1) * ATT_AUG] = k_aug.astype(BF16)


def _att_prep(proj, ccol, batch, seq, tm=512):
    t = proj.shape[0]
    ns = seq // tm
    width = ATT_HEADS * HEAD_DIM
    return pl.pallas_call(
        _attprep_kernel,
        grid=(batch, ns),
        in_specs=[pl.BlockSpec((tm, width), lambda b, i: (b * ns + i, 0)),
                  pl.BlockSpec((tm, width), lambda b, i: (b * ns + i, 1)),
                  pl.BlockSpec((tm, width), lambda b, i: (b * ns + i, 2)),
                  pl.BlockSpec((tm, 128), lambda b, i: (b * ns + i, 0))],
        out_specs=[pl.BlockSpec((tm, ATT_HEADS * ATT_AUG), lambda b, i: (b * ns + i, 0)),
                   pl.BlockSpec((tm, ATT_HEADS * ATT_AUG), lambda b, i: (b * ns + i, 0)),
                   pl.BlockSpec((None, width, tm), lambda b, i: (b, 0, i))],
        out_shape=[jax.ShapeDtypeStruct((t, ATT_HEADS * ATT_AUG), BF16),
                   jax.ShapeDtypeStruct((t, ATT_HEADS * ATT_AUG), BF16),
                   jax.ShapeDtypeStruct((batch, width, seq), BF16)],
        compiler_params=_cparams(("arbitrary", "arbitrary")),
        name="att_prep",
    )(proj, proj, proj, ccol)


def _att_kernel(qa_ref, ka_ref, vt_ref, o_ref, m_ref, l_ref, acc_ref, *, tq, tk, nk):
    i = pl.program_id(1)
    j = pl.program_id(2)

    @pl.when(j == 0)
    def _():
        m_ref[...] = jnp.full(m_ref.shape, NEG_INF, F32)
        l_ref[...] = jnp.zeros(l_ref.shape, F32)
        acc_ref[...] = jnp.zeros(acc_ref.shape, F32)

    def step(masked):
        if masked:
            mask = (lax.broadcasted_iota(I32, (tk, tq), 0) <= lax.broadcasted_iota(I32, (tk, tq), 1))
        def scores(h):
            as_ = slice(h * ATT_AUG, (h + 1) * ATT_AUG)
            s = lax.dot_general(ka_ref[:, as_], qa_ref[:, as_], (((1,), (1,)), ((), ())),
                                preferred_element_type=F32)
            return jnp.where(mask, s, NEG_INF) if masked else s

        s_next = scores(0)
        for h in range(ATT_HEADS):
            hs = slice(h * HEAD_DIM, (h + 1) * HEAD_DIM)
            s = s_next
            if h + 1 < ATT_HEADS:
                s_next = scores(h + 1)
            m_prev = m_ref[h:h + 1, :]
            m_new = jnp.maximum(m_prev, jnp.max(s, axis=0, keepdims=True))
            p = jnp.exp2(s - m_new)
            alpha = jnp.exp2(m_prev - m_new)
            l_ref[h:h + 1, :] = alpha * l_ref[h:h + 1, :] + jnp.sum(p, axis=0, keepdims=True)
            acc_ref[hs, :] = alpha * acc_ref[hs, :] + jnp.dot(vt_ref[hs, :], p.astype(BF16),
                                                            preferred_element_type=F32)
            m_ref[h:h + 1, :] = m_new

    @pl.when(j < i)
    def _():
        step(False)

    @pl.when(j == i)
    def _():
        step(True)

    @pl.when(j == nk - 1)
    def _():
        for h in range(ATT_HEADS):
            hs = slice(h * HEAD_DIM, (h + 1) * HEAD_DIM)
            o_ref[:, hs] = (acc_ref[hs, :] / l_ref[h:h + 1, :]).T


def _attention(qa, ka, vt, batch, seq, tq=512):
    t = batch * seq
    nq = seq // tq
    width = ATT_HEADS * HEAD_DIM
    kernel = functools.partial(_att_kernel, tq=tq, tk=tq, nk=nq)
    return pl.pallas_call(
        kernel,
        grid=(batch, nq, nq),
        in_specs=[pl.BlockSpec((tq, ATT_HEADS * ATT_AUG), lambda b, i, j: (b * nq + i, 0)),
                  pl.BlockSpec((tq, ATT_HEADS * ATT_AUG), lambda b, i, j: (b * nq + jnp.minimum(j, i), 0)),
                  pl.BlockSpec((None, width, tq), lambda b, i, j: (b, 0, jnp.minimum(j, i)))],
        out_specs=pl.BlockSpec((tq, width), lambda b, i, j: (b * nq + i, 0)),
        out_shape=jax.ShapeDtypeStruct((t, width), F32),
        scratch_shapes=[pltpu.VMEM((ATT_HEADS, tq), F32),
                        pltpu.VMEM((ATT_HEADS, tq), F32),
                        pltpu.VMEM((width, tq), F32)],
        compiler_params=_cparams(("arbitrary", "arbitrary", "arbitrary")),
        name="fox_attention",
    )(qa, ka, vt)


def _conv_kernel(b_ref, c_ref, h_ref, w_ref, o_ref, carry_ref, *, ts):
    @pl.when(pl.program_id(1) == 0)
    def _():
        carry_ref[...] = jnp.zeros(carry_ref.shape, F32)

    u = c_ref[...] * h_ref[...]
    ext = jnp.concatenate([carry_ref[...], u], axis=0)
    u1 = ext[7:7 + ts]
    u2 = ext[6:6 + ts]
    w = w_ref[...]
    y = w[2:3] * u + w[1:2] * u1 + w[0:1] * u2
    o_ref[...] = b_ref[...] * y
    carry_ref[...] = u[ts - 8:ts]


def _short_conv(proj, conv_w, batch, seq, ts=512):
    t = batch * seq
    ns = seq // ts
    w = conv_w.shape[1]
    return pl.pallas_call(
        functools.partial(_conv_kernel, ts=ts),
        grid=(batch, ns),
        in_specs=[pl.BlockSpec((ts, w), lambda b, i: (b * ns + i, COL512_CONV_B)),
                  pl.BlockSpec((ts, w), lambda b, i: (b * ns + i, COL512_CONV_C)),
                  pl.BlockSpec((ts, w), lambda b, i: (b * ns + i, COL512_CONV_H)),
                  pl.BlockSpec((CONV_K, w), lambda b, i: (0, 0))],
        out_specs=pl.BlockSpec((ts, w), lambda b, i: (b * ns + i, 0)),
        out_shape=jax.ShapeDtypeStruct((t, w), F32),
        scratch_shapes=[pltpu.VMEM((8, w), F32)],
        compiler_params=_cparams(("arbitrary", "arbitrary")),
        name="short_conv",
    )(proj, proj, proj, conv_w)


def _hgrn_kernel(q_ref, f_ref, i_ref, lb_ref, o_ref, st_ref, qs_ref, kk_ref, bc_ref, *, bs):
    c = HGRN_CHUNK

    @pl.when(pl.program_id(1) == 0)
    def _():
        st_ref[...] = jnp.zeros(st_ref.shape, F32)

    z = f_ref[...]
    lb = lb_ref[...]
    ls = _log_sigmoid(z)
    a = jnp.log(lb)
    b = jnp.log1p(-lb) + ls
    log_f = jnp.maximum(a, b) + jnp.log1p(jnp.exp(-jnp.abs(a - b)))
    kk_ref[...] = (1.0 - lb) * jnp.exp(ls - z)
    q = q_ref[...]
    qs_ref[...] = q / (1.0 + jnp.exp(-q))
    r = lax.broadcasted_iota(I32, (bs, bs), 0)
    cc = lax.broadcasted_iota(I32, (bs, bs), 1)
    tri = ((r // c == cc // c) & (r >= cc)).astype(F32)
    bc_ref[...] = jnp.dot(tri, log_f, precision=lax.Precision.HIGHEST, preferred_element_type=F32)

    t_idx = lax.broadcasted_iota(I32, (c, HEAD_DIM), 0)
    ones = jnp.ones((HEAD_DIM, HEAD_DIM), BF16)

    def chunk(n, carry):
        rows = pl.ds(pl.multiple_of(n * c, c), c)
        for h in range(HGRN_HEADS):
            hs = slice(h * HEAD_DIM, (h + 1) * HEAD_DIM)
            qh = qs_ref[rows, hs]
            kh = kk_ref[rows, hs]
            vh = i_ref[rows, hs]
            bh = bc_ref[rows, hs]
            b_last = bh[c - 1:c]
            parts = []
            for s in range(c):
                d = jnp.where(t_idx >= s, bh - bh[s:s + 1], NEG_INF)
                parts.append(qh * jnp.exp(d) * kh[s:s + 1])
            a3 = jnp.concatenate(parts, axis=0).astype(BF16)
            rs = jnp.dot(a3, ones, preferred_element_type=F32)
            o = lax.dot_general((qh * jnp.exp(bh)).astype(BF16), st_ref[h].astype(BF16),
                                (((1,), (1,)), ((), ())), preferred_element_type=F32)
            for s in range(c):
                o = o + rs[s * c:(s + 1) * c] * vh[s:s + 1]
            o_ref[rows, hs] = o
            kd = kh * jnp.exp(b_last - bh)
            kv_t = lax.dot_general(vh.astype(BF16), kd.astype(BF16), (((0,), (0,)), ((), ())),
                                   preferred_element_type=F32)
            st_ref[h] = jnp.exp(b_last) * st_ref[h] + kv_t
        return carry

    lax.fori_loop(0, bs // c, chunk, 0)


def _hgrn2(proj, lb, batch, seq, bs=256):
    t = batch * seq
    ns = seq // bs
    w = HGRN_HEADS * HEAD_DIM
    return pl.pallas_call(
        functools.partial(_hgrn_kernel, bs=bs),
        grid=(batch, ns),
        in_specs=[pl.BlockSpec((bs, w), lambda b, i: (b * ns + i, COL512_HQ)),
                  pl.BlockSpec((bs, w), lambda b, i: (b * ns + i, COL512_HF)),
                  pl.BlockSpec((bs, w), lambda b, i: (b * ns + i, COL512_HI)),
                  pl.BlockSpec((1, w), lambda b, i: (0, 0))],
        out_specs=pl.BlockSpec((bs, w), lambda b, i: (b * ns + i, 0)),
        out_shape=jax.ShapeDtypeStruct((t, w), F32),
        scratch_shapes=[pltpu.VMEM((HGRN_HEADS, HEAD_DIM, HEAD_DIM), F32),
                        pltpu.VMEM((bs, w), F32),
                        pltpu.VMEM((bs, w), F32),
                        pltpu.VMEM((bs, w), F32)],
        compiler_params=_cparams(("arbitrary", "arbitrary")),
        name="hgrn2",
    )(proj, proj, proj, lb)


def _headnorm_kernel(a_ref, c_ref, r_ref, hg_ref, g_ref, o_ref):
    na = a_ref.shape[1] // HEAD_DIM
    nc = c_ref.shape[1] // HEAD_DIM
    nr = r_ref.shape[1] // HEAD_DIM

    def rms(x, g):
        ms = jnp.mean(x * x, axis=-1, keepdims=True)
        return x * lax.rsqrt(ms + RMS_EPS) * g

    col = 0
    for src, n, gated in ((a_ref, na, False), (c_ref, nc, False), (r_ref, nr, True)):
        for h in range(n):
            hs = slice(h * HEAD_DIM, (h + 1) * HEAD_DIM)
            os_ = slice(col, col + HEAD_DIM)
            y = rms(src[:, hs], g_ref[:, os_])
            if gated:
                hg = hg_ref[:, hs]
                y = y * (hg / (1.0 + jnp.exp(-hg)))
            o_ref[:, os_] = y.astype(o_ref.dtype)
            col += HEAD_DIM


def _head_norm(o_att, o_conv, o_rec, proj, g, tm=256):
    t = o_att.shape[0]
    wa, wc, wr = o_att.shape[1], o_conv.shape[1], o_rec.shape[1]
    d = wa + wc + wr
    return pl.pallas_call(
        _headnorm_kernel,
        grid=(t // tm,),
        in_specs=[pl.BlockSpec((tm, wa), lambda i: (i, 0)),
                  pl.BlockSpec((tm, wc), lambda i: (i, 0)),
                  pl.BlockSpec((tm, wr), lambda i: (i, 0)),
                  pl.BlockSpec((tm, wr), lambda i: (i, COL512_HG)),
                  pl.BlockSpec((1, d), lambda i: (0, 0))],
        out_specs=pl.BlockSpec((tm, d), lambda i: (i, 0)),
        out_shape=jax.ShapeDtypeStruct((t, d), BF16),
        compiler_params=_cparams(("arbitrary",)),
        name="head_norm",
    )(o_att, o_conv, o_rec, proj, g)


def _store_token_tiles(ref, base, y):
    n, d = y.shape
    s = d // 128
    for a in range(s):
        ref[pl.ds(base + a, n, stride=s), :] = y[:, a * 128:(a + 1) * 128]


def _load_token_tiles(ref, base, n, s):
    return [ref[pl.ds(base + a, n, stride=s), :] for a in range(s)]


def _layer_norm_rows(z, g, b):
    mu = jnp.mean(z, axis=-1, keepdims=True)
    zc = z - mu
    var = jnp.mean(zc * zc, axis=-1, keepdims=True)
    return zc * lax.rsqrt(var + LN_EPS) * g + b


def _addln_kernel(x_ref, y_ref, g_ref, b_ref, o_ref, ot_ref, *, alpha):
    y = _layer_norm_rows(alpha * x_ref[...] + y_ref[...], g_ref[...], b_ref[...])
    o_ref[...] = y
    _store_token_tiles(ot_ref, 0, y)


def _add_ln(x, y, g, b, alpha, tm=256):
    t, d = x.shape
    s = d // 128
    row = pl.BlockSpec((tm, d), lambda i: (i, 0))
    vec = pl.BlockSpec((1, d), lambda i: (0, 0))
    return pl.pallas_call(
        functools.partial(_addln_kernel, alpha=alpha),
        grid=(t // tm,),
        in_specs=[row, row, vec, vec],
        out_specs=[row, pl.BlockSpec((tm * s, 128), lambda i: (i, 0))],
        out_shape=[jax.ShapeDtypeStruct((t, d), F32),
                   jax.ShapeDtypeStruct((t * s, 128), F32)],
        compiler_params=_cparams(("arbitrary",)),
        name="add_layernorm",
    )(x, y, g, b)


def _router_kernel(x_ref, w_ref, b_ref, idx_ref, wt_ref):
    logits = jnp.dot(x_ref[...], w_ref[...], precision=lax.Precision.HIGHEST,
                     preferred_element_type=F32) + b_ref[...]
    lane = lax.broadcasted_iota(I32, logits.shape, 1)
    logits = jnp.where(lane < N_EXPERTS, logits, NEG_INF)
    vals, idxs = [], []
    for _ in range(TOP_K):
        m = jnp.max(logits, axis=-1, keepdims=True)
        sel = jnp.min(jnp.where(logits == m, lane, 128), axis=-1, keepdims=True)
        vals.append(m)
        idxs.append(sel)
        logits = jnp.where(lane == sel, NEG_INF, logits)
    es = [jnp.exp(v - vals[0]) for v in vals]
    tot = es[0] + es[1] + es[2] + es[3]
    idx_out = jnp.zeros(logits.shape, I32)
    wt_out = jnp.zeros(logits.shape, F32)
    for k in range(TOP_K):
        idx_out = jnp.where(lane == k, idxs[k], idx_out)
        wt_out = jnp.where(lane == k, es[k] / tot, wt_out)
    idx_ref[...] = idx_out
    wt_ref[...] = wt_out


def _router(x, w_pad, b_pad, tm=512):
    t, d = x.shape
    return pl.pallas_call(
        _router_kernel,
        grid=(t // tm,),
        in_specs=[pl.BlockSpec((tm, d), lambda i: (i, 0)),
                  pl.BlockSpec((d, 128), lambda i: (0, 0)),
                  pl.BlockSpec((1, 128), lambda i: (0, 0))],
        out_specs=[pl.BlockSpec((tm, 128), lambda i: (i, 0)),
                   pl.BlockSpec((tm, 128), lambda i: (i, 0))],
        out_shape=[jax.ShapeDtypeStruct((t, 128), I32),
                   jax.ShapeDtypeStruct((t, 128), F32)],
        compiler_params=_cparams(("arbitrary",)),
        name="router_top4",
    )(x, w_pad, b_pad)


MOE_TM = 256
MOE_TN = 256
MOE_CAP = 1280
MOE_ROW_UNROLL = 8


def _moe_plan(top_idx, n_tokens):
    n_pairs = n_tokens * TOP_K
    g_max = N_EXPERTS + -(-n_pairs // MOE_CAP)
    e_flat = top_idx.reshape(-1)
    onehot = (e_flat[:, None] == jnp.arange(N_EXPERTS, dtype=I32)[None, :]).astype(I32)
    csum = jnp.cumsum(onehot, axis=0)
    rank = jnp.sum(csum * onehot, axis=1) - 1
    counts = csum[-1]
    n_grp = (counts + MOE_CAP - 1) // MOE_CAP
    g_end = jnp.cumsum(n_grp)
    g_base = g_end - n_grp
    pair = jnp.arange(n_pairs, dtype=I32)
    dst = (pair % TOP_K) * n_tokens + pair // TOP_K
    grp = g_base[e_flat] + rank // MOE_CAP
    row = rank % MOE_CAP
    trash = TOP_K * n_tokens + (jnp.arange(MOE_CAP, dtype=I32) % MOE_TM)
    table = jnp.broadcast_to(trash[None, :], (g_max, MOE_CAP))
    table = table.at[grp, row].set(dst)
    gid = jnp.arange(g_max, dtype=I32)
    n_used = g_end[-1]
    g_exp = jnp.sum((jnp.minimum(gid, n_used - 1)[:, None] >= g_end[None, :]).astype(I32), axis=1)
    g_rows = jnp.clip(counts[g_exp] - (gid - g_base[g_exp]) * MOE_CAP, 0, MOE_CAP)
    g_rows = jnp.where(gid < n_used, g_rows, 0)
    g_tiles = ((g_rows + MOE_TM - 1) // MOE_TM).astype(I32)
    return table, g_exp, g_tiles, g_max


def _moe_kernel(ge_ref, gt_ref, tbl_hbm, x_hbm, w1g_ref, w1u_ref, b1g_ref, b1u_ref, w2_ref, b2_ref,
                out_hbm, tbl_smem, xbuf, xb, yacc, ystage, w1g_b, w1u_b, w2_b, sem_tbl, sem_in, sem_out,
                *, n_tokens, n_steps):
    g = pl.program_id(0)
    j = pl.program_id(1)
    nt = gt_ref[g]
    tm = MOE_TM
    d = yacc.shape[1]
    s = d // 128

    def token_rows(r, n=1):
        return pl.ds(pl.multiple_of(r * s, s), n * s)

    def row_copy_in(r):
        tok = tbl_smem[r] & (n_tokens - 1)
        return pltpu.make_async_copy(x_hbm.at[token_rows(tok)], xbuf.at[token_rows(r)], sem_in)

    def row_copy_out(r, slot):
        src = ystage.at[token_rows(slot * tm + (r & (tm - 1)))]
        return pltpu.make_async_copy(src, out_hbm.at[token_rows(tbl_smem[r])], sem_out.at[slot])

    def wait_tile_out(slot):
        pltpu.make_async_copy(ystage.at[token_rows(0, tm)], out_hbm.at[token_rows(0, tm)],
                              sem_out.at[slot]).wait()

    def for_rows(first, n_rows, fn):
        def body(i, c):
            for u in range(MOE_ROW_UNROLL):
                fn(first + i * MOE_ROW_UNROLL + u)
            return c

        lax.fori_loop(0, n_rows // MOE_ROW_UNROLL, body, 0)

    def for_tiles(fn):
        def body(m, c):
            fn(m)
            return c

        lax.fori_loop(0, nt, body, 0)

    def tile_rows(m):
        return pl.ds(pl.multiple_of(m * tm, tm), tm)

    @pl.when((g == 0) & (j == 0))
    def _():
        ystage[pl.ds(0, tm * s), :] = jnp.zeros((tm * s, 128), F32)
        cp = pltpu.make_async_copy(ystage.at[token_rows(0, tm)],
                                   out_hbm.at[token_rows(TOP_K * n_tokens, tm)], sem_out.at[0])
        cp.start()
        cp.wait()

    @pl.when(nt > 0)
    def _():
        @pl.when(j == 0)
        def _():
            cp = pltpu.make_async_copy(tbl_hbm.at[g], tbl_smem, sem_tbl)
            cp.start()
            cp.wait()
            for_rows(0, nt * tm, lambda r: row_copy_in(r).start())

            def init_acc(m):
                yacc[tile_rows(m), :] = jnp.broadcast_to(b2_ref[...], (tm, d))

            for_tiles(init_acc)

            def wait_rows(m):
                pltpu.make_async_copy(x_hbm.at[token_rows(0, tm)], xbuf.at[token_rows(0, tm)], sem_in).wait()

            for_tiles(wait_rows)

            def to_rows(m):
                parts = _load_token_tiles(xbuf, pl.multiple_of(m * tm * s, tm * s), tm, s)
                for a in range(s):
                    xb[tile_rows(m), a * 128:(a + 1) * 128] = parts[a].astype(BF16)

            for_tiles(to_rows)

        w1g_b[...] = w1g_ref[...].astype(BF16)
        w1u_b[...] = w1u_ref[...].astype(BF16)
        w2_b[...] = w2_ref[...].astype(BF16)

        def tile(m):
            rows = tile_rows(m)
            x = xb[rows, :]
            hg = jnp.dot(x, w1g_b[...], preferred_element_type=F32) + b1g_ref[...]
            hu = jnp.dot(x, w1u_b[...], preferred_element_type=F32) + b1u_ref[...]
            gate = jnp.minimum(hg, SWIGLU_LIMIT)
            up = jnp.clip(hu, -SWIGLU_LIMIT, SWIGLU_LIMIT)
            act = (up + 1.0) * gate / (1.0 + jnp.exp(-SWIGLU_ALPHA * gate))
            yacc[rows, :] += jnp.dot(act.astype(BF16), w2_b[...], preferred_element_type=F32)

        def pair(p, c):
            tile(2 * p)
            tile(2 * p + 1)
            return c

        lax.fori_loop(0, nt >> 1, pair, 0)

        @pl.when((nt & 1) == 1)
        def _():
            tile(nt - 1)

        @pl.when(j == n_steps - 1)
        def _():
            def emit(m):
                slot = m & 1

                @pl.when(m >= 2)
                def _():
                    wait_tile_out(slot)

                _store_token_tiles(ystage, pl.multiple_of(slot * tm * s, tm * s), yacc[tile_rows(m), :])
                for_rows(m * tm, tm, lambda r: row_copy_out(r, slot).start())

            for_tiles(emit)

            @pl.when(nt >= 2)
            def _():
                wait_tile_out(nt & 1)

            wait_tile_out((nt - 1) & 1)


def _moe_ffn(xt, table, g_exp, g_tiles, g_max, w1, b1, w2, b2, layer, n_tokens):
    d = w2.shape[3]
    s = d // 128
    d_ff = w2.shape[2]
    n_steps = d_ff // MOE_TN
    tn = MOE_TN

    def jj(g, j, gt):
        return jnp.where(gt[g] > 0, j, n_steps - 1)

    grid_spec = pltpu.PrefetchScalarGridSpec(
        num_scalar_prefetch=2,
        grid=(g_max, n_steps),
        in_specs=[
            pl.BlockSpec(memory_space=pl.ANY),
            pl.BlockSpec(memory_space=pl.ANY),
            pl.BlockSpec((None, None, d, tn), lambda g, j, ge, gt: (layer, ge[g], 0, jj(g, j, gt))),
            pl.BlockSpec((None, None, d, tn), lambda g, j, ge, gt: (layer, ge[g], 0, jj(g, j, gt) + n_steps)),
            pl.BlockSpec((None, None, 1, tn), lambda g, j, ge, gt: (layer, ge[g], 0, jj(g, j, gt))),
            pl.BlockSpec((None, None, 1, tn), lambda g, j, ge, gt: (layer, ge[g], 0, jj(g, j, gt) + n_steps)),
            pl.BlockSpec((None, None, tn, d), lambda g, j, ge, gt: (layer, ge[g], jj(g, j, gt), 0)),
            pl.BlockSpec((None, None, 1, d), lambda g, j, ge, gt: (layer, ge[g], 0, 0)),
        ],
        out_specs=pl.BlockSpec(memory_space=pl.ANY),
        scratch_shapes=[
            pltpu.SMEM((MOE_CAP,), I32),
            pltpu.VMEM((MOE_CAP * s, 128), F32),
            pltpu.VMEM((MOE_CAP, d), BF16),
            pltpu.VMEM((MOE_CAP, d), F32),
            pltpu.VMEM((2 * MOE_TM * s, 128), F32),
            pltpu.VMEM((d, tn), BF16),
            pltpu.VMEM((d, tn), BF16),
            pltpu.VMEM((tn, d), BF16),
            pltpu.SemaphoreType.DMA,
            pltpu.SemaphoreType.DMA,
            pltpu.SemaphoreType.DMA((2,)),
        ],
    )
    return pl.pallas_call(
        functools.partial(_moe_kernel, n_tokens=n_tokens, n_steps=n_steps),
        grid_spec=grid_spec,
        out_shape=jax.ShapeDtypeStruct(((TOP_K * n_tokens + MOE_TM) * s, 128), F32),
        compiler_params=_cparams(("arbitrary", "arbitrary")),
        name="moe_ffn",
    )(g_exp, g_tiles, table, xt, w1, w1, b1, b1, w2, b2)


def _combine_kernel(y0_ref, y1_ref, y2_ref, y3_ref, wt_ref, x_ref, g_ref, b_ref, o_ref, ob_ref, z_ref, *, alpha):
    tm, d = x_ref.shape
    s = d // 128
    wt = wt_ref[...]
    slots = [_load_token_tiles(y_ref, 0, tm, s) for y_ref in (y0_ref, y1_ref, y2_ref, y3_ref)]
    for a in range(s):
        cs = slice(a * 128, (a + 1) * 128)
        y = (wt[:, 0:1] * slots[0][a] + wt[:, 1:2] * slots[1][a]
             + wt[:, 2:3] * slots[2][a] + wt[:, 3:4] * slots[3][a])
        z_ref[:, cs] = alpha * x_ref[:, cs] + y
    z = _layer_norm_rows(z_ref[...], g_ref[...], b_ref[...])
    o_ref[...] = z
    ob_ref[...] = z.astype(BF16)


def _combine_ln(y4, wts, x, g, b, alpha, tm=256):
    t, d = x.shape
    s = d // 128
    nb = t // tm
    row = pl.BlockSpec((tm, d), lambda i: (i, 0))
    vec = pl.BlockSpec((1, d), lambda i: (0, 0))
    slot = [pl.BlockSpec((tm * s, 128), functools.partial(lambda i, k: (k * nb + i, 0), k=k))
            for k in range(TOP_K)]
    return pl.pallas_call(
        functools.partial(_combine_kernel, alpha=alpha),
        grid=(nb,),
        in_specs=slot + [pl.BlockSpec((tm, 128), lambda i: (i, 0)), row, vec, vec],
        out_specs=[row, row],
        out_shape=[jax.ShapeDtypeStruct((t, d), F32), jax.ShapeDtypeStruct((t, d), BF16)],
        scratch_shapes=[pltpu.VMEM((tm, d), F32)],
        compiler_params=_cparams(("arbitrary",)),
        name="combine_layernorm",
    )(y4, y4, y4, y4, wts, x, g, b)


def _pack_w_in(w_in_l):
    d = w_in_l.shape[0]
    att = ATT_HEADS * HEAD_DIM * 3
    n_f = ATT_HEADS
    conv = 3 * 4 * HEAD_DIM
    return jnp.concatenate([
        w_in_l[:, :att],
        w_in_l[:, att + n_f:att + n_f + conv],
        w_in_l[:, att:att + n_f],
        jnp.zeros((d, 512 - n_f), F32),
        w_in_l[:, att + n_f + conv:],
    ], axis=1)


def kernel(x, w_in, b_fgate, conv_w, lower_bounds, head_norm_g, w_out, ln1_g, ln1_b, w_router, b_router,
           w1, b1, w2, b2, ln2_g, ln2_b):
    batch, seq, d = x.shape
    depth = w_in.shape[0]
    t = batch * seq
    alpha = (2 * depth) ** 0.25

    lb = jnp.cumsum(jax.nn.softmax(lower_bounds.astype(F32), axis=0), axis=0)
    lb = lb - lb[0]
    b1r = b1.reshape(depth, N_EXPERTS, 1, b1.shape[-1])
    b2r = b2.reshape(depth, N_EXPERTS, 1, b2.shape[-1])

    xf = x.reshape(t, d)
    xb = xf.astype(BF16)
    for l in range(depth):
        proj = _matmul(xb, _pack_w_in(w_in[l]), 512, 1024, F32, "in_proj")
        bf_pad = jnp.zeros((1, 128), F32).at[0, :ATT_HEADS].set(b_fgate[l])
        ccol = _forget_cumsum(proj, bf_pad, batch, seq)
        qa, ka, vt = _att_prep(proj, ccol, batch, seq)
        o_att = _attention(qa, ka, vt, batch, seq)
        o_conv = _short_conv(proj, conv_w[l], batch, seq)
        o_rec = _hgrn2(proj, lb[l][None, :], batch, seq)
        o = _head_norm(o_att, o_conv, o_rec, proj, head_norm_g[l][None, :])
        mix = _matmul(o, w_out[l], 512, 1024, F32, "out_proj")
        x1, x1t = _add_ln(xf, mix, ln1_g[l][None, :], ln1_b[l][None, :], alpha)

        wr_pad = jnp.zeros((d, 128), F32).at[:, :N_EXPERTS].set(w_router[l])
        br_pad = jnp.zeros((1, 128), F32).at[0, :N_EXPERTS].set(b_router[l])
        idx, wts = _router(x1, wr_pad, br_pad)
        table, g_exp, g_tiles, g_max = _moe_plan(idx[:, :TOP_K], t)
        y4 = _moe_ffn(x1t, table, g_exp, g_tiles, g_max, w1, b1r, w2, b2r, l, t)
        xf, xb = _combine_ln(y4, wts, x1, ln2_g[l][None, :], ln2_b[l][None, :], alpha)
    return xf.reshape(batch, seq, d)
```

```python
import functools

import jax
import jax.numpy as jnp
import numpy as np
from jax import lax
from jax.experimental import pallas as pl
from jax.experimental.pallas import tpu as pltpu

F32 = jnp.float32
BF16 = jnp.bfloat16
I32 = jnp.int32
U32 = jnp.uint32

HEAD_DIM = 128
ATT_HEADS = 8
CONV_K = 3
HGRN_HEADS = 4
HGRN_CHUNK = 16
N_EXPERTS = 32
TOP_K = 4
SWIGLU_LIMIT = 7.0
SWIGLU_ALPHA = 1.702
LN_EPS = 1e-5
RMS_EPS = 1e-6
NEG_INF = float("-inf")
LOG2_E = 1.4426950408889634

V7X_VMEM_BYTES = 64 * 1024 * 1024
VMEM_LIMIT = 56 * 1024 * 1024

PROJ_COLS = 7168
COL_F = 36
COL512_CONV_B, COL512_CONV_C, COL512_CONV_H = 6, 7, 8
COL512_HQ, COL512_HF, COL512_HI, COL512_HG = 10, 11, 12, 13


def _cparams(sem):
    return pltpu.CompilerParams(dimension_semantics=sem, vmem_limit_bytes=VMEM_LIMIT)


def _log_sigmoid(z):
    return jnp.minimum(z, 0.0) - jnp.log1p(jnp.exp(-jnp.abs(z)))


def _mm_kernel(x_ref, w_ref, o_ref, wb_ref):
    @pl.when(pl.program_id(1) == 0)
    def _():
        wb_ref[...] = w_ref[...].astype(BF16)

    o_ref[...] = jnp.dot(x_ref[...].astype(BF16), wb_ref[...],
                         preferred_element_type=F32).astype(o_ref.dtype)


def _matmul(x, w, tm, tn, out_dtype, name):
    m, k = x.shape
    n = w.shape[1]
    return pl.pallas_call(
        _mm_kernel,
        grid=(n // tn, m // tm),
        in_specs=[pl.BlockSpec((tm, k), lambda j, i: (i, 0)),
                  pl.BlockSpec((k, tn), lambda j, i: (0, j))],
        out_specs=pl.BlockSpec((tm, tn), lambda j, i: (i, j)),
        out_shape=jax.ShapeDtypeStruct((m, n), out_dtype),
        scratch_shapes=[pltpu.VMEM((k, tn), BF16)],
        compiler_params=_cparams(("arbitrary", "arbitrary")),
        name=name,
    )(x, w)


def _fcum_kernel(f_ref, b_ref, ccol_ref, *, seq, blk):
    ls = _log_sigmoid(f_ref[...] + b_ref[...])
    r = lax.broadcasted_iota(I32, (blk, blk), 0)
    c = lax.broadcasted_iota(I32, (blk, blk), 1)
    tri = (r >= c).astype(F32)
    carry = jnp.zeros((1, 128), F32)
    for i in range(seq // blk):
        cs = jnp.dot(tri, ls[i * blk:(i + 1) * blk], precision=lax.Precision.HIGHEST,
                     preferred_element_type=F32) + carry
        ccol_ref[i * blk:(i + 1) * blk, :] = cs
        carry = cs[blk - 1:blk, :]


def _forget_cumsum(proj, b_fgate_pad, batch, seq):
    t = batch * seq
    return pl.pallas_call(
        functools.partial(_fcum_kernel, seq=seq, blk=256),
        grid=(batch,),
        in_specs=[pl.BlockSpec((seq, 128), lambda b: (b, COL_F)),
                  pl.BlockSpec((1, 128), lambda b: (0, 0))],
        out_specs=pl.BlockSpec((seq, 128), lambda b: (b, 0)),
        out_shape=jax.ShapeDtypeStruct((t, 128), F32),
        compiler_params=_cparams(("arbitrary",)),
        name="forget_cumsum",
    )(proj, b_fgate_pad)


ATT_AUG = 2 * HEAD_DIM


def _split3_bf16(c):
    hi = c.astype(BF16).astype(F32)
    r1 = c - hi
    mid = r1.astype(BF16).astype(F32)
    lo = (r1 - mid).astype(BF16).astype(F32)
    return hi, mid, lo


def _attprep_kernel(q_ref, k_ref, v_ref, cc_ref, qa_ref, ka_ref, vb_ref):
    scale = HEAD_DIM ** -0.5 * LOG2_E
    tm = q_ref.shape[0]
    lane = lax.broadcasted_iota(I32, (tm, HEAD_DIM), 1)
    vb_ref[...] = v_ref[...].T.astype(BF16)
    for h in range(ATT_HEADS):
        hs = slice(h * HEAD_DIM, (h + 1) * HEAD_DIM)
        c = jnp.broadcast_to(cc_ref[:, h:h + 1] * LOG2_E, (tm, HEAD_DIM))
        hi, mid, lo = _split3_bf16(c)
        one = jnp.ones_like(c)
        zero = jnp.zeros_like(c)
        q_aug = jnp.where(lane == 0, hi, jnp.where(lane == 1, mid, jnp.where(lane == 2, lo,
                          jnp.where(lane < 6, one, zero))))
        k_aug = jnp.where(lane < 3, one, jnp.where(lane == 3, -hi, jnp.where(lane == 4, -mid,
                          jnp.where(lane == 5, -lo, zero))))
        qa_ref[:, h * ATT_AUG:h * ATT_AUG + HEAD_DIM] = (q_ref[:, hs] * scale).astype(BF16)
        qa_ref[:, h * ATT_AUG + HEAD_DIM:(h + 1) * ATT_AUG] = q_aug.astype(BF16)
        ka_ref[:, h * ATT_AUG:h * ATT_AUG + HEAD_DIM] = k_ref[:, hs].astype(BF16)
        ka_ref[:, h * ATT_AUG + HEAD_DIM:(h + 1) * ATT_AUG] = k_aug.astype(BF16)


def _att_prep(proj, ccol, batch, seq, tm=512):
    t = proj.shape[0]
    ns = seq // tm
    width = ATT_HEADS * HEAD_DIM
    return pl.pallas_call(
        _attprep_kernel,
        grid=(batch, ns),
        in_specs=[pl.BlockSpec((tm, width), lambda b, i: (b * ns + i, 0)),
                  pl.BlockSpec((tm, width), lambda b, i: (b * ns + i, 1)),
                  pl.BlockSpec((tm, width), lambda b, i: (b * ns + i, 2)),
                  pl.BlockSpec((tm, 128), lambda b, i: (b * ns + i, 0))],
        out_specs=[pl.BlockSpec((tm, ATT_HEADS * ATT_AUG), lambda b, i: (b * ns + i, 0)),
                   pl.BlockSpec((tm, ATT_HEADS * ATT_AUG), lambda b, i: (b * ns + i, 0)),
                   pl.BlockSpec((None, width, tm), lambda b, i: (b, 0, i))],
        out_shape=[jax.ShapeDtypeStruct((t, ATT_HEADS * ATT_AUG), BF16),
                   jax.ShapeDtypeStruct((t, ATT_HEADS * ATT_AUG), BF16),
                   jax.ShapeDtypeStruct((batch, width, seq), BF16)],
        compiler_params=_cparams(("arbitrary", "arbitrary")),
        name="att_prep",
    )(proj, proj, proj, ccol)


def _att_kernel(qa_ref, ka_ref, vt_ref, o_ref, m_ref, l_ref, acc_ref, *, tq, tk, nk):
    i = pl.program_id(1)
    j = pl.program_id(2)

    @pl.when(j == 0)
    def _():
        m_ref[...] = jnp.full(m_ref.shape, NEG_INF, F32)
        l_ref[...] = jnp.zeros(l_ref.shape, F32)
        acc_ref[...] = jnp.zeros(acc_ref.shape, F32)

    def step(masked):
        if masked:
            mask = (lax.broadcasted_iota(I32, (tk, tq), 0) <= lax.broadcasted_iota(I32, (tk, tq), 1))
        def scores(h):
            as_ = slice(h * ATT_AUG, (h + 1) * ATT_AUG)
            s = lax.dot_general(ka_ref[:, as_], qa_ref[:, as_], (((1,), (1,)), ((), ())),
                                preferred_element_type=F32)
            return jnp.where(mask, s, NEG_INF) if masked else s

        s_next = scores(0)
        for h in range(ATT_HEADS):
            hs = slice(h * HEAD_DIM, (h + 1) * HEAD_DIM)
            s = s_next
            if h + 1 < ATT_HEADS:
                s_next = scores(h + 1)
            m_prev = m_ref[h:h + 1, :]
            m_new = jnp.maximum(m_prev, jnp.max(s, axis=0, keepdims=True))
            p = jnp.exp2(s - m_new)
            alpha = jnp.exp2(m_prev - m_new)
            l_ref[h:h + 1, :] = alpha * l_ref[h:h + 1, :] + jnp.sum(p, axis=0, keepdims=True)
            acc_ref[hs, :] = alpha * acc_ref[hs, :] + jnp.dot(vt_ref[hs, :], p.astype(BF16),
                                                            preferred_element_type=F32)
            m_ref[h:h + 1, :] = m_new

    @pl.when(j < i)
    def _():
        step(False)

    @pl.when(j == i)
    def _():
        step(True)

    @pl.when(j == nk - 1)
    def _():
        for h in range(ATT_HEADS):
            hs = slice(h * HEAD_DIM, (h + 1) * HEAD_DIM)
            o_ref[:, hs] = (acc_ref[hs, :] / l_ref[h:h + 1, :]).T


def _attention(qa, ka, vt, batch, seq, tq=512):
    t = batch * seq
    nq = seq // tq
    width = ATT_HEADS * HEAD_DIM
    kernel = functools.partial(_att_kernel, tq=tq, tk=tq, nk=nq)
    return pl.pallas_call(
        kernel,
        grid=(batch, nq, nq),
        in_specs=[pl.BlockSpec((tq, ATT_HEADS * ATT_AUG), lambda b, i, j: (b * nq + i, 0)),
                  pl.BlockSpec((tq, ATT_HEADS * ATT_AUG), lambda b, i, j: (b * nq + jnp.minimum(j, i), 0)),
                  pl.BlockSpec((None, width, tq), lambda b, i, j: (b, 0, jnp.minimum(j, i)))],
        out_specs=pl.BlockSpec((tq, width), lambda b, i, j: (b * nq + i, 0)),
        out_shape=jax.ShapeDtypeStruct((t, width), F32),
        scratch_shapes=[pltpu.VMEM((ATT_HEADS, tq), F32),
                        pltpu.VMEM((ATT_HEADS, tq), F32),
                        pltpu.VMEM((width, tq), F32)],
        compiler_params=_cparams(("arbitrary", "arbitrary", "arbitrary")),
        name="fox_attention",
    )(qa, ka, vt)


def _conv_kernel(b_ref, c_ref, h_ref, w_ref, o_ref, carry_ref, *, ts):
    @pl.when(pl.program_id(1) == 0)
    def _():
        carry_ref[...] = jnp.zeros(carry_ref.shape, F32)

    u = c_ref[...] * h_ref[...]
    ext = jnp.concatenate([carry_ref[...], u], axis=0)
    u1 = ext[7:7 + ts]
    u2 = ext[6:6 + ts]
    w = w_ref[...]
    y = w[2:3] * u + w[1:2] * u1 + w[0:1] * u2
    o_ref[...] = b_ref[...] * y
    carry_ref[...] = u[ts - 8:ts]


def _short_conv(proj, conv_w, batch, seq, ts=512):
    t = batch * seq
    ns = seq // ts
    w = conv_w.shape[1]
    return pl.pallas_call(
        functools.partial(_conv_kernel, ts=ts),
        grid=(batch, ns),
        in_specs=[pl.BlockSpec((ts, w), lambda b, i: (b * ns + i, COL512_CONV_B)),
                  pl.BlockSpec((ts, w), lambda b, i: (b * ns + i, COL512_CONV_C)),
                  pl.BlockSpec((ts, w), lambda b, i: (b * ns + i, COL512_CONV_H)),
                  pl.BlockSpec((CONV_K, w), lambda b, i: (0, 0))],
        out_specs=pl.BlockSpec((ts, w), lambda b, i: (b * ns + i, 0)),
        out_shape=jax.ShapeDtypeStruct((t, w), F32),
        scratch_shapes=[pltpu.VMEM((8, w), F32)],
        compiler_params=_cparams(("arbitrary", "arbitrary")),
        name="short_conv",
    )(proj, proj, proj, conv_w)


def _hgrn_kernel(q_ref, f_ref, i_ref, lb_ref, o_ref, st_ref, qs_ref, kk_ref, bc_ref, *, bs):
    c = HGRN_CHUNK

    @pl.when(pl.program_id(1) == 0)
    def _():
        st_ref[...] = jnp.zeros(st_ref.shape, F32)

    z = f_ref[...]
    lb = lb_ref[...]
    ls = _log_sigmoid(z)
    a = jnp.log(lb)
    b = jnp.log1p(-lb) + ls
    log_f = jnp.maximum(a, b) + jnp.log1p(jnp.exp(-jnp.abs(a - b)))
    kk_ref[...] = (1.0 - lb) * jnp.exp(ls - z)
    q = q_ref[...]
    qs_ref[...] = q / (1.0 + jnp.exp(-q))
    r = lax.broadcasted_iota(I32, (bs, bs), 0)
    cc = lax.broadcasted_iota(I32, (bs, bs), 1)
    tri = ((r // c == cc // c) & (r >= cc)).astype(F32)
    bc_ref[...] = jnp.dot(tri, log_f, precision=lax.Precision.HIGHEST, preferred_element_type=F32)

    t_idx = lax.broadcasted_iota(I32, (c, HEAD_DIM), 0)
    ones = jnp.ones((HEAD_DIM, HEAD_DIM), BF16)

    def chunk(n, carry):
        rows = pl.ds(pl.multiple_of(n * c, c), c)
        for h in range(HGRN_HEADS):
            hs = slice(h * HEAD_DIM, (h + 1) * HEAD_DIM)
            qh = qs_ref[rows, hs]
            kh = kk_ref[rows, hs]
            vh = i_ref[rows, hs]
            bh = bc_ref[rows, hs]
            b_last = bh[c - 1:c]
            parts = []
            for s in range(c):
                d = jnp.where(t_idx >= s, bh - bh[s:s + 1], NEG_INF)
                parts.append(qh * jnp.exp(d) * kh[s:s + 1])
            a3 = jnp.concatenate(parts, axis=0).astype(BF16)
            rs = jnp.dot(a3, ones, preferred_element_type=F32)
            o = lax.dot_general((qh * jnp.exp(bh)).astype(BF16), st_ref[h].astype(BF16),
                                (((1,), (1,)), ((), ())), preferred_element_type=F32)
            for s in range(c):
                o = o + rs[s * c:(s + 1) * c] * vh[s:s + 1]
            o_ref[rows, hs] = o
            kd = kh * jnp.exp(b_last - bh)
            kv_t = lax.dot_general(vh.astype(BF16), kd.astype(BF16), (((0,), (0,)), ((), ())),
                                   preferred_element_type=F32)
            st_ref[h] = jnp.exp(b_last) * st_ref[h] + kv_t
        return carry

    lax.fori_loop(0, bs // c, chunk, 0)


def _hgrn2(proj, lb, batch, seq, bs=256):
    t = batch * seq
    ns = seq // bs
    w = HGRN_HEADS * HEAD_DIM
    return pl.pallas_call(
        functools.partial(_hgrn_kernel, bs=bs),
        grid=(batch, ns),
        in_specs=[pl.BlockSpec((bs, w), lambda b, i: (b * ns + i, COL512_HQ)),
                  pl.BlockSpec((bs, w), lambda b, i: (b * ns + i, COL512_HF)),
                  pl.BlockSpec((bs, w), lambda b, i: (b * ns + i, COL512_HI)),
                  pl.BlockSpec((1, w), lambda b, i: (0, 0))],
        out_specs=pl.BlockSpec((bs, w), lambda b, i: (b * ns + i, 0)),
        out_shape=jax.ShapeDtypeStruct((t, w), F32),
        scratch_shapes=[pltpu.VMEM((HGRN_HEADS, HEAD_DIM, HEAD_DIM), F32),
                        pltpu.VMEM((bs, w), F32),
                        pltpu.VMEM((bs, w), F32),
                        pltpu.VMEM((bs, w), F32)],
        compiler_params=_cparams(("arbitrary", "arbitrary")),
        name="hgrn2",
    )(proj, proj, proj, lb)


def _headnorm_kernel(a_ref, c_ref, r_ref, hg_ref, g_ref, o_ref):
    na = a_ref.shape[1] // HEAD_DIM
    nc = c_ref.shape[1] // HEAD_DIM
    nr = r_ref.shape[1] // HEAD_DIM

    def rms(x, g):
        ms = jnp.mean(x * x, axis=-1, keepdims=True)
        return x * lax.rsqrt(ms + RMS_EPS) * g

    col = 0
    for src, n, gated in ((a_ref, na, False), (c_ref, nc, False), (r_ref, nr, True)):
        for h in range(n):
            hs = slice(h * HEAD_DIM, (h + 1) * HEAD_DIM)
            os_ = slice(col, col + HEAD_DIM)
            y = rms(src[:, hs], g_ref[:, os_])
            if gated:
                hg = hg_ref[:, hs]
                y = y * (hg / (1.0 + jnp.exp(-hg)))
            o_ref[:, os_] = y.astype(o_ref.dtype)
            col += HEAD_DIM


def _head_norm(o_att, o_conv, o_rec, proj, g, tm=256):
    t = o_att.shape[0]
    wa, wc, wr = o_att.shape[1], o_conv.shape[1], o_rec.shape[1]
    d = wa + wc + wr
    return pl.pallas_call(
        _headnorm_kernel,
        grid=(t // tm,),
        in_specs=[pl.BlockSpec((tm, wa), lambda i: (i, 0)),
                  pl.BlockSpec((tm, wc), lambda i: (i, 0)),
                  pl.BlockSpec((tm, wr), lambda i: (i, 0)),
                  pl.BlockSpec((tm, wr), lambda i: (i, COL512_HG)),
                  pl.BlockSpec((1, d), lambda i: (0, 0))],
        out_specs=pl.BlockSpec((tm, d), lambda i: (i, 0)),
        out_shape=jax.ShapeDtypeStruct((t, d), BF16),
        compiler_params=_cparams(("arbitrary",)),
        name="head_norm",
    )(o_att, o_conv, o_rec, proj, g)


def _store_token_tiles(ref, base, y):
    n, d = y.shape
    s = d // 128
    for a in range(s):
        ref[pl.ds(base + a, n, stride=s), :] = y[:, a * 128:(a + 1) * 128]


def _load_token_tiles(ref, base, n, s):
    return [ref[pl.ds(base + a, n, stride=s), :] for a in range(s)]


def _layer_norm_rows(z, g, b):
    mu = jnp.mean(z, axis=-1, keepdims=True)
    zc = z - mu
    var = jnp.mean(zc * zc, axis=-1, keepdims=True)
    return zc * lax.rsqrt(var + LN_EPS) * g + b


def _addln_kernel(x_ref, y_ref, g_ref, b_ref, o_ref, ot_ref, *, alpha):
    y = _layer_norm_rows(alpha * x_ref[...] + y_ref[...], g_ref[...], b_ref[...])
    o_ref[...] = y
    _store_token_tiles(ot_ref, 0, y)


def _add_ln(x, y, g, b, alpha, tm=256):
    t, d = x.shape
    s = d // 128
    row = pl.BlockSpec((tm, d), lambda i: (i, 0))
    vec = pl.BlockSpec((1, d), lambda i: (0, 0))
    return pl.pallas_call(
        functools.partial(_addln_kernel, alpha=alpha),
        grid=(t // tm,),
        in_specs=[row, row, vec, vec],
        out_specs=[row, pl.BlockSpec((tm * s, 128), lambda i: (i, 0))],
        out_shape=[jax.ShapeDtypeStruct((t, d), F32),
                   jax.ShapeDtypeStruct((t * s, 128), F32)],
        compiler_params=_cparams(("arbitrary",)),
        name="add_layernorm",
    )(x, y, g, b)


def _router_kernel(x_ref, w_ref, b_ref, idx_ref, wt_ref):
    logits = jnp.dot(x_ref[...], w_ref[...], precision=lax.Precision.HIGHEST,
                     preferred_element_type=F32) + b_ref[...]
    lane = lax.broadcasted_iota(I32, logits.shape, 1)
    logits = jnp.where(lane < N_EXPERTS, logits, NEG_INF)
    vals, idxs = [], []
    for _ in range(TOP_K):
        m = jnp.max(logits, axis=-1, keepdims=True)
        sel = jnp.min(jnp.where(logits == m, lane, 128), axis=-1, keepdims=True)
        vals.append(m)
        idxs.append(sel)
        logits = jnp.where(lane == sel, NEG_INF, logits)
    es = [jnp.exp(v - vals[0]) for v in vals]
    tot = es[0] + es[1] + es[2] + es[3]
    idx_out = jnp.zeros(logits.shape, I32)
    wt_out = jnp.zeros(logits.shape, F32)
    for k in range(TOP_K):
        idx_out = jnp.where(lane == k, idxs[k], idx_out)
        wt_out = jnp.where(lane == k, es[k] / tot, wt_out)
    idx_ref[...] = idx_out
    wt_ref[...] = wt_out


def _router(x, w_pad, b_pad, tm=512):
    t, d = x.shape
    return pl.pallas_call(
        _router_kernel,
        grid=(t // tm,),
        in_specs=[pl.BlockSpec((tm, d), lambda i: (i, 0)),
                  pl.BlockSpec((d, 128), lambda i: (0, 0)),
                  pl.BlockSpec((1, 128), lambda i: (0, 0))],
        out_specs=[pl.BlockSpec((tm, 128), lambda i: (i, 0)),
                   pl.BlockSpec((tm, 128), lambda i: (i, 0))],
        out_shape=[jax.ShapeDtypeStruct((t, 128), I32),
                   jax.ShapeDtypeStruct((t, 128), F32)],
        compiler_params=_cparams(("arbitrary",)),
        name="router_top4",
    )(x, w_pad, b_pad)


MOE_TM = 256
MOE_TILES_PER_ITER = 2
MOE_TN = 256
MOE_CAP = 1280
MOE_ROW_UNROLL = 8


def _moe_plan(top_idx, n_tokens):
    n_pairs = n_tokens * TOP_K
    g_max = N_EXPERTS + -(-n_pairs // MOE_CAP)
    e_flat = top_idx.reshape(-1)
    onehot = (e_flat[:, None] == jnp.arange(N_EXPERTS, dtype=I32)[None, :]).astype(I32)
    csum = jnp.cumsum(onehot, axis=0)
    rank = jnp.sum(csum * onehot, axis=1) - 1
    counts = csum[-1]
    n_grp = (counts + MOE_CAP - 1) // MOE_CAP
    g_end = jnp.cumsum(n_grp)
    g_base = g_end - n_grp
    pair = jnp.arange(n_pairs, dtype=I32)
    dst = (pair % TOP_K) * n_tokens + pair // TOP_K
    grp = g_base[e_flat] + rank // MOE_CAP
    row = rank % MOE_CAP
    trash = TOP_K * n_tokens + (jnp.arange(MOE_CAP, dtype=I32) % MOE_TM)
    table = jnp.broadcast_to(trash[None, :], (g_max, MOE_CAP))
    table = table.at[grp, row].set(dst)
    gid = jnp.arange(g_max, dtype=I32)
    n_used = g_end[-1]
    g_exp = jnp.sum((jnp.minimum(gid, n_used - 1)[:, None] >= g_end[None, :]).astype(I32), axis=1)
    g_rows = jnp.clip(counts[g_exp] - (gid - g_base[g_exp]) * MOE_CAP, 0, MOE_CAP)
    g_rows = jnp.where(gid < n_used, g_rows, 0)
    g_tiles = ((g_rows + MOE_TM - 1) // MOE_TM).astype(I32)
    return table, g_exp, g_tiles, g_max


def _moe_kernel(ge_ref, gt_ref, tbl_hbm, x_hbm, w1g_ref, w1u_ref, b1g_ref, b1u_ref, w2_ref, b2_ref,
                out_hbm, tbl_smem, xbuf, xb, yacc, ystage, w1g_b, w1u_b, w2_b, sem_tbl, sem_in, sem_out,
                *, n_tokens, n_steps):
    g = pl.program_id(0)
    j = pl.program_id(1)
    n_groups = pl.num_programs(0)
    nt = gt_ref[g]
    nt_prev = gt_ref[jnp.maximum(g - 1, 0)]
    g_next = jnp.minimum(g + 1, n_groups - 1)
    tm = MOE_TM
    d = yacc.shape[1]
    s = d // 128
    tbase = (g & 1) * MOE_CAP

    def token_rows(r, n=1):
        return pl.ds(pl.multiple_of(r * s, s), n * s)

    def start_gather(grp, n_tiles):
        base = (grp & 1) * MOE_CAP
        cp = pltpu.make_async_copy(tbl_hbm.at[grp], tbl_smem.at[pl.ds(base, MOE_CAP)], sem_tbl)
        cp.start()
        cp.wait()

        def row_copy_in(r):
            tok = tbl_smem[base + r] & (n_tokens - 1)
            pltpu.make_async_copy(x_hbm.at[token_rows(tok)], xbuf.at[token_rows(r)], sem_in).start()

        for_rows(0, n_tiles * tm, row_copy_in)

    def row_copy_out(r, slot):
        src = ystage.at[token_rows(slot * tm + (r & (tm - 1)))]
        return pltpu.make_async_copy(src, out_hbm.at[token_rows(tbl_smem[tbase + r])], sem_out.at[slot])

    def wait_tile_out(slot):
        pltpu.make_async_copy(ystage.at[token_rows(0, tm)], out_hbm.at[token_rows(0, tm)],
                              sem_out.at[slot]).wait()

    def drain_scatter(n_tiles):
        @pl.when(n_tiles >= 2)
        def _():
            wait_tile_out(n_tiles & 1)

        wait_tile_out((n_tiles - 1) & 1)

    def for_rows(first, n_rows, fn):
        def body(i, c):
            for u in range(MOE_ROW_UNROLL):
                fn(first + i * MOE_ROW_UNROLL + u)
            return c

        lax.fori_loop(0, n_rows // MOE_ROW_UNROLL, body, 0)

    def for_tiles(fn):
        def body(m, c):
            fn(m)
            return c

        lax.fori_loop(0, nt, body, 0)

    def tile_rows(m):
        return pl.ds(pl.multiple_of(m * tm, tm), tm)

    @pl.when((g == 0) & (j == 0))
    def _():
        ystage[pl.ds(0, tm * s), :] = jnp.zeros((tm * s, 128), F32)
        cp = pltpu.make_async_copy(ystage.at[token_rows(0, tm)],
                                   out_hbm.at[token_rows(TOP_K * n_tokens, tm)], sem_out.at[0])
        cp.start()
        cp.wait()
        start_gather(g, nt)

    @pl.when((nt == 0) & (j == 0) & (g > 0) & (nt_prev > 0))
    def _():
        drain_scatter(nt_prev)

    @pl.when(nt > 0)
    def _():
        @pl.when(j == 0)
        def _():
            def init_acc(m):
                yacc[tile_rows(m), :] = jnp.broadcast_to(b2_ref[...], (tm, d))

            for_tiles(init_acc)

            def wait_rows(m):
                pltpu.make_async_copy(x_hbm.at[token_rows(0, tm)], xbuf.at[token_rows(0, tm)], sem_in).wait()

            for_tiles(wait_rows)

            def to_rows(m):
                parts = _load_token_tiles(xbuf, pl.multiple_of(m * tm * s, tm * s), tm, s)
                for a in range(s):
                    xb[tile_rows(m), a * 128:(a + 1) * 128] = parts[a].astype(BF16)

            for_tiles(to_rows)

            @pl.when((g + 1 < n_groups) & (gt_ref[g_next] > 0))
            def _():
                start_gather(g_next, gt_ref[g_next])

        w1g_b[...] = w1g_ref[...].astype(BF16)
        w1u_b[...] = w1u_ref[...].astype(BF16)
        w2_b[...] = w2_ref[...].astype(BF16)

        def tile(m):
            rows = tile_rows(m)
            x = xb[rows, :]
            hg = jnp.dot(x, w1g_b[...], preferred_element_type=F32) + b1g_ref[...]
            hu = jnp.dot(x, w1u_b[...], preferred_element_type=F32) + b1u_ref[...]
            gate = jnp.minimum(hg, SWIGLU_LIMIT)
            up = jnp.clip(hu, -SWIGLU_LIMIT, SWIGLU_LIMIT)
            act = (up + 1.0) * gate / (1.0 + jnp.exp(-SWIGLU_ALPHA * gate))
            yacc[rows, :] += jnp.dot(act.astype(BF16), w2_b[...], preferred_element_type=F32)

        def multi(p, c):
            for u in range(MOE_TILES_PER_ITER):
                tile(p * MOE_TILES_PER_ITER + u)
            return c

        n_multi = nt // MOE_TILES_PER_ITER
        lax.fori_loop(0, n_multi, multi, 0)

        def single(m, c):
            tile(m)
            return c

        lax.fori_loop(n_multi * MOE_TILES_PER_ITER, nt, single, 0)

        @pl.when(j == n_steps - 1)
        def _():
            @pl.when(g > 0)
            def _():
                drain_scatter(nt_prev)

            def emit(m):
                slot = m & 1

                @pl.when(m >= 2)
                def _():
                    wait_tile_out(slot)

                _store_token_tiles(ystage, pl.multiple_of(slot * tm * s, tm * s), yacc[tile_rows(m), :])
                for_rows(m * tm, tm, lambda r: row_copy_out(r, slot).start())

            for_tiles(emit)


def _moe_ffn(xt, table, g_exp, g_tiles, g_max, w1, b1, w2, b2, layer, n_tokens):
    d = w2.shape[3]
    s = d // 128
    d_ff = w2.shape[2]
    n_steps = d_ff // MOE_TN
    tn = MOE_TN

    def jj(g, j, gt):
        return jnp.where(gt[g] > 0, j, n_steps - 1)

    grid_spec = pltpu.PrefetchScalarGridSpec(
        num_scalar_prefetch=2,
        grid=(g_max, n_steps),
        in_specs=[
            pl.BlockSpec(memory_space=pl.ANY),
            pl.BlockSpec(memory_space=pl.ANY),
            pl.BlockSpec((None, None, d, tn), lambda g, j, ge, gt: (layer, ge[g], 0, jj(g, j, gt))),
            pl.BlockSpec((None, None, d, tn), lambda g, j, ge, gt: (layer, ge[g], 0, jj(g, j, gt) + n_steps)),
            pl.BlockSpec((None, None, 1, tn), lambda g, j, ge, gt: (layer, ge[g], 0, jj(g, j, gt))),
            pl.BlockSpec((None, None, 1, tn), lambda g, j, ge, gt: (layer, ge[g], 0, jj(g, j, gt) + n_steps)),
            pl.BlockSpec((None, None, tn, d), lambda g, j, ge, gt: (layer, ge[g], jj(g, j, gt), 0)),
            pl.BlockSpec((None, None, 1, d), lambda g, j, ge, gt: (layer, ge[g], 0, 0)),
        ],
        out_specs=pl.BlockSpec(memory_space=pl.ANY),
        scratch_shapes=[
            pltpu.SMEM((2 * MOE_CAP,), I32),
            pltpu.VMEM((MOE_CAP * s, 128), F32),
            pltpu.VMEM((MOE_CAP, d), BF16),
            pltpu.VMEM((MOE_CAP, d), F32),
            pltpu.VMEM((2 * MOE_TM * s, 128), F32),
            pltpu.VMEM((d, tn), BF16),
            pltpu.VMEM((d, tn), BF16),
            pltpu.VMEM((tn, d), BF16),
            pltpu.SemaphoreType.DMA,
            pltpu.SemaphoreType.DMA,
            pltpu.SemaphoreType.DMA((2,)),
        ],
    )
    return pl.pallas_call(
        functools.partial(_moe_kernel, n_tokens=n_tokens, n_steps=n_steps),
        grid_spec=grid_spec,
        out_shape=jax.ShapeDtypeStruct(((TOP_K * n_tokens + MOE_TM) * s, 128), F32),
        compiler_params=_cparams(("arbitrary", "arbitrary")),
        name="moe_ffn",
    )(g_exp, g_tiles, table, xt, w1, w1, b1, b1, w2, b2)


def _combine_kernel(y0_ref, y1_ref, y2_ref, y3_ref, wt_ref, x_ref, g_ref, b_ref, o_ref, ob_ref, z_ref, *, alpha):
    tm, d = x_ref.shape
    s = d // 128
    wt = wt_ref[...]
    slots = [_load_token_tiles(y_ref, 0, tm, s) for y_ref in (y0_ref, y1_ref, y2_ref, y3_ref)]
    for a in range(s):
        cs = slice(a * 128, (a + 1) * 128)
        y = (wt[:, 0:1] * slots[0][a] + wt[:, 1:2] * slots[1][a]
             + wt[:, 2:3] * slots[2][a] + wt[:, 3:4] * slots[3][a])
        z_ref[:, cs] = alpha * x_ref[:, cs] + y
    z = _layer_norm_rows(z_ref[...], g_ref[...], b_ref[...])
    o_ref[...] = z
    ob_ref[...] = z.astype(BF16)


def _combine_ln(y4, wts, x, g, b, alpha, tm=256):
    t, d = x.shape
    s = d // 128
    nb = t // tm
    row = pl.BlockSpec((tm, d), lambda i: (i, 0))
    vec = pl.BlockSpec((1, d), lambda i: (0, 0))
    slot = [pl.BlockSpec((tm * s, 128), functools.partial(lambda i, k: (k * nb + i, 0), k=k))
            for k in range(TOP_K)]
    return pl.pallas_call(
        functools.partial(_combine_kernel, alpha=alpha),
        grid=(nb,),
        in_specs=slot + [pl.BlockSpec((tm, 128), lambda i: (i, 0)), row, vec, vec],
        out_specs=[row, row],
        out_shape=[jax.ShapeDtypeStruct((t, d), F32), jax.ShapeDtypeStruct((t, d), BF16)],
        scratch_shapes=[pltpu.VMEM((tm, d), F32)],
        compiler_params=_cparams(("arbitrary",)),
        name="combine_layernorm",
    )(y4, y4, y4, y4, wts, x, g, b)


def _pack_w_in(w_in_l):
    d = w_in_l.shape[0]
    att = ATT_HEADS * HEAD_DIM * 3
    n_f = ATT_HEADS
    conv = 3 * 4 * HEAD_DIM
    return jnp.concatenate([
        w_in_l[:, :att],
        w_in_l[:, att + n_f:att + n_f + conv],
        w_in_l[:, att:att + n_f],
        jnp.zeros((d, 512 - n_f), F32),
        w_in_l[:, att + n_f + conv:],
    ], axis=1)


def kernel(x, w_in, b_fgate, conv_w, lower_bounds, head_norm_g, w_out, ln1_g, ln1_b, w_router, b_router,
           w1, b1, w2, b2, ln2_g, ln2_b):
    batch, seq, d = x.shape
    depth = w_in.shape[0]
    t = batch * seq
    alpha = (2 * depth) ** 0.25

    lb = jnp.cumsum(jax.nn.softmax(lower_bounds.astype(F32), axis=0), axis=0)
    lb = lb - lb[0]
    b1r = b1.reshape(depth, N_EXPERTS, 1, b1.shape[-1])
    b2r = b2.reshape(depth, N_EXPERTS, 1, b2.shape[-1])

    xf = x.reshape(t, d)
    xb = xf.astype(BF16)
    for l in range(depth):
        proj = _matmul(xb, _pack_w_in(w_in[l]), 512, 1024, F32, "in_proj")
        bf_pad = jnp.zeros((1, 128), F32).at[0, :ATT_HEADS].set(b_fgate[l])
        ccol = _forget_cumsum(proj, bf_pad, batch, seq)
        qa, ka, vt = _att_prep(proj, ccol, batch, seq)
        o_att = _attention(qa, ka, vt, batch, seq)
        o_conv = _short_conv(proj, conv_w[l], batch, seq)
        o_rec = _hgrn2(proj, lb[l][None, :], batch, seq)
        o = _head_norm(o_att, o_conv, o_rec, proj, head_norm_g[l][None, :])
        mix = _matmul(o, w_out[l], 512, 1024, F32, "out_proj")
        x1, x1t = _add_ln(xf, mix, ln1_g[l][None, :], ln1_b[l][None, :], alpha)

        wr_pad = jnp.zeros((d, 128), F32).at[:, :N_EXPERTS].set(w_router[l])
        br_pad = jnp.zeros((1, 128), F32).at[0, :N_EXPERTS].set(b_router[l])
        idx, wts = _router(x1, wr_pad, br_pad)
        table, g_exp, g_tiles, g_max = _moe_plan(idx[:, :TOP_K], t)
        y4 = _moe_ffn(x1t, table, g_exp, g_tiles, g_max, w1, b1r, w2, b2r, l, t)
        xf, xb = _combine_ln(y4, wts, x1, ln2_g[l][None, :], ln2_b[l][None, :], alpha)
    return xf.reshape(batch, seq, d)
```

```python
import functools

import jax
import jax.numpy as jnp
import numpy as np
from jax import lax
from jax.experimental import pallas as pl
from jax.experimental.pallas import tpu as pltpu

F32 = jnp.float32
BF16 = jnp.bfloat16
I32 = jnp.int32
U32 = jnp.uint32

HEAD_DIM = 128
ATT_HEADS = 8
CONV_K = 3
HGRN_HEADS = 4
HGRN_CHUNK = 16
N_EXPERTS = 32
TOP_K = 4
SWIGLU_LIMIT = 7.0
SWIGLU_ALPHA = 1.702
LN_EPS = 1e-5
RMS_EPS = 1e-6
NEG_INF = float("-inf")
LOG2_E = 1.4426950408889634

V7X_VMEM_BYTES = 64 * 1024 * 1024
VMEM_LIMIT = 56 * 1024 * 1024

PROJ_COLS = 7168
COL_F = 36
COL512_CONV_B, COL512_CONV_C, COL512_CONV_H = 6, 7, 8
COL512_HQ, COL512_HF, COL512_HI, COL512_HG = 10, 11, 12, 13


def _cparams(sem):
    return pltpu.CompilerParams(dimension_semantics=sem, vmem_limit_bytes=VMEM_LIMIT)


def _log_sigmoid(z):
    return jnp.minimum(z, 0.0) - jnp.log1p(jnp.exp(-jnp.abs(z)))


def _mm_kernel(x_ref, w_ref, o_ref, wb_ref):
    @pl.when(pl.program_id(1) == 0)
    def _():
        wb_ref[...] = w_ref[...].astype(BF16)

    o_ref[...] = jnp.dot(x_ref[...].astype(BF16), wb_ref[...],
                         preferred_element_type=F32).astype(o_ref.dtype)


def _matmul(x, w, tm, tn, out_dtype, name):
    m, k = x.shape
    n = w.shape[1]
    return pl.pallas_call(
        _mm_kernel,
        grid=(n // tn, m // tm),
        in_specs=[pl.BlockSpec((tm, k), lambda j, i: (i, 0)),
                  pl.BlockSpec((k, tn), lambda j, i: (0, j))],
        out_specs=pl.BlockSpec((tm, tn), lambda j, i: (i, j)),
        out_shape=jax.ShapeDtypeStruct((m, n), out_dtype),
        scratch_shapes=[pltpu.VMEM((k, tn), BF16)],
        compiler_params=_cparams(("arbitrary", "arbitrary")),
        name=name,
    )(x, w)


def _fcum_kernel(f_ref, b_ref, ccol_ref, *, seq, blk):
    ls = _log_sigmoid(f_ref[...] + b_ref[...])
    r = lax.broadcasted_iota(I32, (blk, blk), 0)
    c = lax.broadcasted_iota(I32, (blk, blk), 1)
    tri = (r >= c).astype(F32)
    carry = jnp.zeros((1, 128), F32)
    for i in range(seq // blk):
        cs = jnp.dot(tri, ls[i * blk:(i + 1) * blk], precision=lax.Precision.HIGHEST,
                     preferred_element_type=F32) + carry
        ccol_ref[i * blk:(i + 1) * blk, :] = cs
        carry = cs[blk - 1:blk, :]


def _forget_cumsum(proj, b_fgate_pad, batch, seq):
    t = batch * seq
    return pl.pallas_call(
        functools.partial(_fcum_kernel, seq=seq, blk=256),
        grid=(batch,),
        in_specs=[pl.BlockSpec((seq, 128), lambda b: (b, COL_F)),
                  pl.BlockSpec((1, 128), lambda b: (0, 0))],
        out_specs=pl.BlockSpec((seq, 128), lambda b: (b, 0)),
        out_shape=jax.ShapeDtypeStruct((t, 128), F32),
        compiler_params=_cparams(("arbitrary",)),
        name="forget_cumsum",
    )(proj, b_fgate_pad)


ATT_AUG = 2 * HEAD_DIM


def _split3_bf16(c):
    hi = c.astype(BF16).astype(F32)
    r1 = c - hi
    mid = r1.astype(BF16).astype(F32)
    lo = (r1 - mid).astype(BF16).astype(F32)
    return hi, mid, lo


def _attprep_kernel(q_ref, k_ref, v_ref, cc_ref, qa_ref, ka_ref, vb_ref):
    scale = HEAD_DIM ** -0.5 * LOG2_E
    tm = q_ref.shape[0]
    lane = lax.broadcasted_iota(I32, (tm, HEAD_DIM), 1)
    vb_ref[...] = v_ref[...].T.astype(BF16)
    for h in range(ATT_HEADS):
        hs = slice(h * HEAD_DIM, (h + 1) * HEAD_DIM)
        c = jnp.broadcast_to(cc_ref[:, h:h + 1] * LOG2_E, (tm, HEAD_DIM))
        hi, mid, lo = _split3_bf16(c)
        one = jnp.ones_like(c)
        zero = jnp.zeros_like(c)
        q_aug = jnp.where(lane == 0, hi, jnp.where(lane == 1, mid, jnp.where(lane == 2, lo,
                          jnp.where(lane < 6, one, zero))))
        k_aug = jnp.where(lane < 3, one, jnp.where(lane == 3, -hi, jnp.where(lane == 4, -mid,
                          jnp.where(lane == 5, -lo, zero))))
        qa_ref[:, h * ATT_AUG:h * ATT_AUG + HEAD_DIM] = (q_ref[:, hs] * scale).astype(BF16)
        qa_ref[:, h * ATT_AUG + HEAD_DIM:(h + 1) * ATT_AUG] = q_aug.astype(BF16)
        ka_ref[:, h * ATT_AUG:h * ATT_AUG + HEAD_DIM] = k_ref[:, hs].astype(BF16)
        ka_ref[:, h * ATT_AUG + HEAD_DIM:(h + 1) * ATT_AUG] = k_aug.astype(BF16)


def _att_prep(proj, ccol, batch, seq, tm=512):
    t = proj.shape[0]
    ns = seq // tm
    width = ATT_HEADS * HEAD_DIM
    return pl.pallas_call(
        _attprep_kernel,
        grid=(batch, ns),
        in_specs=[pl.BlockSpec((tm, width), lambda b, i: (b * ns + i, 0)),
                  pl.BlockSpec((tm, width), lambda b, i: (b * ns + i, 1)),
                  pl.BlockSpec((tm, width), lambda b, i: (b * ns + i, 2)),
                  pl.BlockSpec((tm, 128), lambda b, i: (b * ns + i, 0))],
        out_specs=[pl.BlockSpec((tm, ATT_HEADS * ATT_AUG), lambda b, i: (b * ns + i, 0)),
                   pl.BlockSpec((tm, ATT_HEADS * ATT_AUG), lambda b, i: (b * ns + i, 0)),
                   pl.BlockSpec((None, width, tm), lambda b, i: (b, 0, i))],
        out_shape=[jax.ShapeDtypeStruct((t, ATT_HEADS * ATT_AUG), BF16),
                   jax.ShapeDtypeStruct((t, ATT_HEADS * ATT_AUG), BF16),
                   jax.ShapeDtypeStruct((batch, width, seq), BF16)],
        compiler_params=_cparams(("arbitrary", "arbitrary")),
        name="att_prep",
    )(proj, proj, proj, ccol)


def _att_kernel(qa_ref, ka_ref, vt_ref, o_ref, m_ref, l_ref, acc_ref, *, tq, tk, nk):
    i = pl.program_id(1)
    j = pl.program_id(2)

    @pl.when(j == 0)
    def _():
        m_ref[...] = jnp.full(m_ref.shape, NEG_INF, F32)
        l_ref[...] = jnp.zeros(l_ref.shape, F32)
        acc_ref[...] = jnp.zeros(acc_ref.shape, F32)

    def step(masked):
        if masked:
            mask = (lax.broadcasted_iota(I32, (tk, tq), 0) <= lax.broadcasted_iota(I32, (tk, tq), 1))
        def scores(h):
            as_ = slice(h * ATT_AUG, (h + 1) * ATT_AUG)
            s = lax.dot_general(ka_ref[:, as_], qa_ref[:, as_], (((1,), (1,)), ((), ())),
                                preferred_element_type=F32)
            return jnp.where(mask, s, NEG_INF) if masked else s

        s_next = scores(0)
        for h in range(ATT_HEADS):
            hs = slice(h * HEAD_DIM, (h + 1) * HEAD_DIM)
            s = s_next
            if h + 1 < ATT_HEADS:
                s_next = scores(h + 1)
            m_prev = m_ref[h:h + 1, :]
            m_new = jnp.maximum(m_prev, jnp.max(s, axis=0, keepdims=True))
            p = jnp.exp2(s - m_new)
            alpha = jnp.exp2(m_prev - m_new)
            l_ref[h:h + 1, :] = alpha * l_ref[h:h + 1, :] + jnp.sum(p, axis=0, keepdims=True)
            acc_ref[hs, :] = alpha * acc_ref[hs, :] + jnp.dot(vt_ref[hs, :], p.astype(BF16),
                                                            preferred_element_type=F32)
            m_ref[h:h + 1, :] = m_new

    @pl.when(j < i)
    def _():
        step(False)

    @pl.when(j == i)
    def _():
        step(True)

    @pl.when(j == nk - 1)
    def _():
        for h in range(ATT_HEADS):
            hs = slice(h * HEAD_DIM, (h + 1) * HEAD_DIM)
            o_ref[:, hs] = (acc_ref[hs, :] / l_ref[h:h + 1, :]).T


def _attention(qa, ka, vt, batch, seq, tq=512):
    t = batch * seq
    nq = seq // tq
    width = ATT_HEADS * HEAD_DIM
    kernel = functools.partial(_att_kernel, tq=tq, tk=tq, nk=nq)
    return pl.pallas_call(
        kernel,
        grid=(batch, nq, nq),
        in_specs=[pl.BlockSpec((tq, ATT_HEADS * ATT_AUG), lambda b, i, j: (b * nq + i, 0)),
                  pl.BlockSpec((tq, ATT_HEADS * ATT_AUG), lambda b, i, j: (b * nq + jnp.minimum(j, i), 0)),
                  pl.BlockSpec((None, width, tq), lambda b, i, j: (b, 0, jnp.minimum(j, i)))],
        out_specs=pl.BlockSpec((tq, width), lambda b, i, j: (b * nq + i, 0)),
        out_shape=jax.ShapeDtypeStruct((t, width), F32),
        scratch_shapes=[pltpu.VMEM((ATT_HEADS, tq), F32),
                        pltpu.VMEM((ATT_HEADS, tq), F32),
                        pltpu.VMEM((width, tq), F32)],
        compiler_params=_cparams(("arbitrary", "arbitrary", "arbitrary")),
        name="fox_attention",
    )(qa, ka, vt)


def _conv_kernel(b_ref, c_ref, h_ref, w_ref, o_ref, carry_ref, *, ts):
    @pl.when(pl.program_id(1) == 0)
    def _():
        carry_ref[...] = jnp.zeros(carry_ref.shape, F32)

    u = c_ref[...] * h_ref[...]
    ext = jnp.concatenate([carry_ref[...], u], axis=0)
    u1 = ext[7:7 + ts]
    u2 = ext[6:6 + ts]
    w = w_ref[...]
    y = w[2:3] * u + w[1:2] * u1 + w[0:1] * u2
    o_ref[...] = b_ref[...] * y
    carry_ref[...] = u[ts - 8:ts]


def _short_conv(proj, conv_w, batch, seq, ts=512):
    t = batch * seq
    ns = seq // ts
    w = conv_w.shape[1]
    return pl.pallas_call(
        functools.partial(_conv_kernel, ts=ts),
        grid=(batch, ns),
        in_specs=[pl.BlockSpec((ts, w), lambda b, i: (b * ns + i, COL512_CONV_B)),
                  pl.BlockSpec((ts, w), lambda b, i: (b * ns + i, COL512_CONV_C)),
                  pl.BlockSpec((ts, w), lambda b, i: (b * ns + i, COL512_CONV_H)),
                  pl.BlockSpec((CONV_K, w), lambda b, i: (0, 0))],
        out_specs=pl.BlockSpec((ts, w), lambda b, i: (b * ns + i, 0)),
        out_shape=jax.ShapeDtypeStruct((t, w), F32),
        scratch_shapes=[pltpu.VMEM((8, w), F32)],
        compiler_params=_cparams(("arbitrary", "arbitrary")),
        name="short_conv",
    )(proj, proj, proj, conv_w)


def _hgrn_kernel(q_ref, f_ref, i_ref, lb_ref, o_ref, st_ref, qs_ref, kk_ref, bc_ref, *, bs):
    c = HGRN_CHUNK

    @pl.when(pl.program_id(1) == 0)
    def _():
        st_ref[...] = jnp.zeros(st_ref.shape, F32)

    z = f_ref[...]
    lb = lb_ref[...]
    ls = _log_sigmoid(z)
    a = jnp.log(lb)
    b = jnp.log1p(-lb) + ls
    log_f = jnp.maximum(a, b) + jnp.log1p(jnp.exp(-jnp.abs(a - b)))
    kk_ref[...] = (1.0 - lb) * jnp.exp(ls - z)
    q = q_ref[...]
    qs_ref[...] = q / (1.0 + jnp.exp(-q))
    r = lax.broadcasted_iota(I32, (bs, bs), 0)
    cc = lax.broadcasted_iota(I32, (bs, bs), 1)
    tri = ((r // c == cc // c) & (r >= cc)).astype(F32)
    bc_ref[...] = jnp.dot(tri, log_f, precision=lax.Precision.HIGHEST, preferred_element_type=F32)

    t_idx = lax.broadcasted_iota(I32, (c, HEAD_DIM), 0)
    ones = jnp.ones((HEAD_DIM, HEAD_DIM), BF16)

    def chunk(n, carry):
        rows = pl.ds(pl.multiple_of(n * c, c), c)
        for h in range(HGRN_HEADS):
            hs = slice(h * HEAD_DIM, (h + 1) * HEAD_DIM)
            qh = qs_ref[rows, hs]
            kh = kk_ref[rows, hs]
            vh = i_ref[rows, hs]
            bh = bc_ref[rows, hs]
            b_last = bh[c - 1:c]
            parts = []
            for s in range(c):
                d = jnp.where(t_idx >= s, bh - bh[s:s + 1], NEG_INF)
                parts.append(qh * jnp.exp(d) * kh[s:s + 1])
            a3 = jnp.concatenate(parts, axis=0).astype(BF16)
            rs = jnp.dot(a3, ones, preferred_element_type=F32)
            o = lax.dot_general((qh * jnp.exp(bh)).astype(BF16), st_ref[h].astype(BF16),
                                (((1,), (1,)), ((), ())), preferred_element_type=F32)
            for s in range(c):
                o = o + rs[s * c:(s + 1) * c] * vh[s:s + 1]
            o_ref[rows, hs] = o
            kd = kh * jnp.exp(b_last - bh)
            kv_t = lax.dot_general(vh.astype(BF16), kd.astype(BF16), (((0,), (0,)), ((), ())),
                                   preferred_element_type=F32)
            st_ref[h] = jnp.exp(b_last) * st_ref[h] + kv_t
        return carry

    lax.fori_loop(0, bs // c, chunk, 0)


def _hgrn2(proj, lb, batch, seq, bs=256):
    t = batch * seq
    ns = seq // bs
    w = HGRN_HEADS * HEAD_DIM
    return pl.pallas_call(
        functools.partial(_hgrn_kernel, bs=bs),
        grid=(batch, ns),
        in_specs=[pl.BlockSpec((bs, w), lambda b, i: (b * ns + i, COL512_HQ)),
                  pl.BlockSpec((bs, w), lambda b, i: (b * ns + i, COL512_HF)),
                  pl.BlockSpec((bs, w), lambda b, i: (b * ns + i, COL512_HI)),
                  pl.BlockSpec((1, w), lambda b, i: (0, 0))],
        out_specs=pl.BlockSpec((bs, w), lambda b, i: (b * ns + i, 0)),
        out_shape=jax.ShapeDtypeStruct((t, w), F32),
        scratch_shapes=[pltpu.VMEM((HGRN_HEADS, HEAD_DIM, HEAD_DIM), F32),
                        pltpu.VMEM((bs, w), F32),
                        pltpu.VMEM((bs, w), F32),
                        pltpu.VMEM((bs, w), F32)],
        compiler_params=_cparams(("arbitrary", "arbitrary")),
        name="hgrn2",
    )(proj, proj, proj, lb)


def _headnorm_kernel(a_ref, c_ref, r_ref, hg_ref, g_ref, o_ref):
    na = a_ref.shape[1] // HEAD_DIM
    nc = c_ref.shape[1] // HEAD_DIM
    nr = r_ref.shape[1] // HEAD_DIM

    def rms(x, g):
        ms = jnp.mean(x * x, axis=-1, keepdims=True)
        return x * lax.rsqrt(ms + RMS_EPS) * g

    col = 0
    for src, n, gated in ((a_ref, na, False), (c_ref, nc, False), (r_ref, nr, True)):
        for h in range(n):
            hs = slice(h * HEAD_DIM, (h + 1) * HEAD_DIM)
            os_ = slice(col, col + HEAD_DIM)
            y = rms(src[:, hs], g_ref[:, os_])
            if gated:
                hg = hg_ref[:, hs]
                y = y * (hg / (1.0 + jnp.exp(-hg)))
            o_ref[:, os_] = y.astype(o_ref.dtype)
            col += HEAD_DIM


def _head_norm(o_att, o_conv, o_rec, proj, g, tm=256):
    t = o_att.shape[0]
    wa, wc, wr = o_att.shape[1], o_conv.shape[1], o_rec.shape[1]
    d = wa + wc + wr
    return pl.pallas_call(
        _headnorm_kernel,
        grid=(t // tm,),
        in_specs=[pl.BlockSpec((tm, wa), lambda i: (i, 0)),
                  pl.BlockSpec((tm, wc), lambda i: (i, 0)),
                  pl.BlockSpec((tm, wr), lambda i: (i, 0)),
                  pl.BlockSpec((tm, wr), lambda i: (i, COL512_HG)),
                  pl.BlockSpec((1, d), lambda i: (0, 0))],
        out_specs=pl.BlockSpec((tm, d), lambda i: (i, 0)),
        out_shape=jax.ShapeDtypeStruct((t, d), BF16),
        compiler_params=_cparams(("arbitrary",)),
        name="head_norm",
    )(o_att, o_conv, o_rec, proj, g)


def _store_token_tiles(ref, base, y, pitch=None):
    n, d = y.shape
    s = d // 128
    for a in range(s):
        ref[pl.ds(base + a, n, stride=pitch or s), :] = y[:, a * 128:(a + 1) * 128]


def _load_token_tiles(ref, base, n, s, pitch=None):
    return [ref[pl.ds(base + a, n, stride=pitch or s), :] for a in range(s)]


def _layer_norm_rows(z, g, b):
    mu = jnp.mean(z, axis=-1, keepdims=True)
    zc = z - mu
    var = jnp.mean(zc * zc, axis=-1, keepdims=True)
    return zc * lax.rsqrt(var + LN_EPS) * g + b


def _addln_kernel(x_ref, y_ref, g_ref, b_ref, o_ref, ot_ref, *, alpha):
    y = _layer_norm_rows(alpha * x_ref[...] + y_ref[...], g_ref[...], b_ref[...])
    o_ref[...] = y
    _store_token_tiles(ot_ref, 0, y)


def _add_ln(x, y, g, b, alpha, tm=256):
    t, d = x.shape
    s = d // 128
    row = pl.BlockSpec((tm, d), lambda i: (i, 0))
    vec = pl.BlockSpec((1, d), lambda i: (0, 0))
    return pl.pallas_call(
        functools.partial(_addln_kernel, alpha=alpha),
        grid=(t // tm,),
        in_specs=[row, row, vec, vec],
        out_specs=[row, pl.BlockSpec((tm * s, 128), lambda i: (i, 0))],
        out_shape=[jax.ShapeDtypeStruct((t, d), F32),
                   jax.ShapeDtypeStruct((t * s, 128), F32)],
        compiler_params=_cparams(("arbitrary",)),
        name="add_layernorm",
    )(x, y, g, b)


def _router_kernel(x_ref, w_ref, b_ref, idx_ref, wt_ref):
    logits = jnp.dot(x_ref[...], w_ref[...], precision=lax.Precision.HIGHEST,
                     preferred_element_type=F32) + b_ref[...]
    lane = lax.broadcasted_iota(I32, logits.shape, 1)
    logits = jnp.where(lane < N_EXPERTS, logits, NEG_INF)
    vals, idxs = [], []
    for _ in range(TOP_K):
        m = jnp.max(logits, axis=-1, keepdims=True)
        sel = jnp.min(jnp.where(logits == m, lane, 128), axis=-1, keepdims=True)
        vals.append(m)
        idxs.append(sel)
        logits = jnp.where(lane == sel, NEG_INF, logits)
    es = [jnp.exp(v - vals[0]) for v in vals]
    tot = es[0] + es[1] + es[2] + es[3]
    idx_out = jnp.zeros(logits.shape, I32)
    wt_out = jnp.zeros(logits.shape, F32)
    for k in range(TOP_K):
        idx_out = jnp.where(lane == k, idxs[k], idx_out)
        wt_out = jnp.where(lane == k, es[k] / tot, wt_out)
    idx_ref[...] = idx_out
    wt_ref[...] = wt_out


def _router(x, w_pad, b_pad, tm=512):
    t, d = x.shape
    return pl.pallas_call(
        _router_kernel,
        grid=(t // tm,),
        in_specs=[pl.BlockSpec((tm, d), lambda i: (i, 0)),
                  pl.BlockSpec((d, 128), lambda i: (0, 0)),
                  pl.BlockSpec((1, 128), lambda i: (0, 0))],
        out_specs=[pl.BlockSpec((tm, 128), lambda i: (i, 0)),
                   pl.BlockSpec((tm, 128), lambda i: (i, 0))],
        out_shape=[jax.ShapeDtypeStruct((t, 128), I32),
                   jax.ShapeDtypeStruct((t, 128), F32)],
        compiler_params=_cparams(("arbitrary",)),
        name="router_top4",
    )(x, w_pad, b_pad)


MOE_UNIT = 128
MOE_TM = 256
MOE_TILES_PER_ITER = 2
MOE_TN = 256
MOE_CAP = 1280
MOE_ROW_UNROLL = 8
MOE_VMEM_PITCH = 20
MOE_VMEM_PITCH_ALIGN = 4


def _moe_plan(top_idx, n_tokens):
    n_pairs = n_tokens * TOP_K
    g_max = N_EXPERTS + -(-n_pairs // MOE_CAP)
    e_flat = top_idx.reshape(-1)
    onehot = (e_flat[:, None] == jnp.arange(N_EXPERTS, dtype=I32)[None, :]).astype(I32)
    csum = jnp.cumsum(onehot, axis=0)
    rank = jnp.sum(csum * onehot, axis=1) - 1
    counts = csum[-1]
    n_grp = (counts + MOE_CAP - 1) // MOE_CAP
    g_end = jnp.cumsum(n_grp)
    g_base = g_end - n_grp
    pair = jnp.arange(n_pairs, dtype=I32)
    dst = (pair % TOP_K) * n_tokens + pair // TOP_K
    grp = g_base[e_flat] + rank // MOE_CAP
    row = rank % MOE_CAP
    trash = TOP_K * n_tokens + (jnp.arange(MOE_CAP, dtype=I32) % MOE_UNIT)
    table = jnp.broadcast_to(trash[None, :], (g_max, MOE_CAP))
    table = table.at[grp, row].set(dst)
    gid = jnp.arange(g_max, dtype=I32)
    n_used = g_end[-1]
    g_exp = jnp.sum((jnp.minimum(gid, n_used - 1)[:, None] >= g_end[None, :]).astype(I32), axis=1)
    g_rows = jnp.clip(counts[g_exp] - (gid - g_base[g_exp]) * MOE_CAP, 0, MOE_CAP)
    g_rows = jnp.where(gid < n_used, g_rows, 0)
    g_units = ((g_rows + MOE_UNIT - 1) // MOE_UNIT).astype(I32)
    return table, g_exp, g_units, g_max


def _moe_kernel(ge_ref, gt_ref, tbl_hbm, x_hbm, w1g_ref, w1u_ref, b1g_ref, b1u_ref, w2_ref, b2_ref,
                out_hbm, tbl_smem, xbuf, xb, yacc, ystage, w1g_b, w1u_b, w2_b, sem_tbl, sem_in, sem_out,
                *, n_tokens, n_steps):
    g = pl.program_id(0)
    j = pl.program_id(1)
    n_groups = pl.num_programs(0)
    nt = gt_ref[g]
    nt_prev = gt_ref[jnp.maximum(g - 1, 0)]
    g_next = jnp.minimum(g + 1, n_groups - 1)
    tm = MOE_UNIT
    d = yacc.shape[1]
    s = d // 128
    tbase = (g & 1) * MOE_CAP

    pitch = MOE_VMEM_PITCH

    def token_rows(r, n=1):
        return pl.ds(pl.multiple_of(r * s, s), n * s)

    def vmem_token(r):
        return pl.ds(pl.multiple_of(r * pitch, MOE_VMEM_PITCH_ALIGN), s)

    def start_gather(grp, n_tiles):
        base = (grp & 1) * MOE_CAP
        cp = pltpu.make_async_copy(tbl_hbm.at[grp], tbl_smem.at[pl.ds(base, MOE_CAP)], sem_tbl)
        cp.start()
        cp.wait()

        def row_copy_in(r):
            tok = tbl_smem[base + r] & (n_tokens - 1)
            pltpu.make_async_copy(x_hbm.at[token_rows(tok)], xbuf.at[vmem_token(r)], sem_in).start()

        for_rows(0, n_tiles * tm, row_copy_in)

    def row_copy_out(r, slot):
        src = ystage.at[vmem_token(slot * tm + (r & (tm - 1)))]
        return pltpu.make_async_copy(src, out_hbm.at[token_rows(tbl_smem[tbase + r])], sem_out.at[slot])

    def wait_tile_out(slot):
        pltpu.make_async_copy(ystage.at[token_rows(0, tm)], out_hbm.at[token_rows(0, tm)],
                              sem_out.at[slot]).wait()

    def drain_scatter(n_tiles):
        @pl.when(n_tiles >= 2)
        def _():
            wait_tile_out(n_tiles & 1)

        wait_tile_out((n_tiles - 1) & 1)

    def for_rows(first, n_rows, fn):
        def body(i, c):
            for u in range(MOE_ROW_UNROLL):
                fn(first + i * MOE_ROW_UNROLL + u)
            return c

        lax.fori_loop(0, n_rows // MOE_ROW_UNROLL, body, 0)

    def for_tiles(fn):
        def body(m, c):
            fn(m)
            return c

        lax.fori_loop(0, nt, body, 0)

    def tile_rows(m):
        return pl.ds(pl.multiple_of(m * tm, tm), tm)

    @pl.when((g == 0) & (j == 0))
    def _():
        ystage[pl.ds(0, tm * s), :] = jnp.zeros((tm * s, 128), F32)
        cp = pltpu.make_async_copy(ystage.at[token_rows(0, tm)],
                                   out_hbm.at[token_rows(TOP_K * n_tokens, tm)], sem_out.at[0])
        cp.start()
        cp.wait()
        start_gather(g, nt)

    @pl.when((nt == 0) & (j == 0) & (g > 0) & (nt_prev > 0))
    def _():
        drain_scatter(nt_prev)

    @pl.when(nt > 0)
    def _():
        @pl.when(j == 0)
        def _():
            def init_acc(m):
                yacc[tile_rows(m), :] = jnp.broadcast_to(b2_ref[...], (tm, d))

            for_tiles(init_acc)

            def wait_rows(m):
                pltpu.make_async_copy(x_hbm.at[token_rows(0, tm)], xbuf.at[token_rows(0, tm)], sem_in).wait()

            for_tiles(wait_rows)

            def to_rows(m):
                parts = _load_token_tiles(xbuf, pl.multiple_of(m * tm * pitch, tm * pitch), tm, s, pitch)
                for a in range(s):
                    xb[tile_rows(m), a * 128:(a + 1) * 128] = parts[a].astype(BF16)

            for_tiles(to_rows)

            @pl.when((g + 1 < n_groups) & (gt_ref[g_next] > 0))
            def _():
                start_gather(g_next, gt_ref[g_next])

        w1g_b[...] = w1g_ref[...].astype(BF16)
        w1u_b[...] = w1u_ref[...].astype(BF16)
        w2_b[...] = w2_ref[...].astype(BF16)

        def mm_tile(m, n_rows):
            rows = pl.ds(pl.multiple_of(m * MOE_TM, MOE_TM), n_rows)
            x = xb[rows, :]
            hg = jnp.dot(x, w1g_b[...], preferred_element_type=F32) + b1g_ref[...]
            hu = jnp.dot(x, w1u_b[...], preferred_element_type=F32) + b1u_ref[...]
            gate = jnp.minimum(hg, SWIGLU_LIMIT)
            up = jnp.clip(hu, -SWIGLU_LIMIT, SWIGLU_LIMIT)
            act = (up + 1.0) * gate / (1.0 + jnp.exp(-SWIGLU_ALPHA * gate))
            yacc[rows, :] += jnp.dot(act.astype(BF16), w2_b[...], preferred_element_type=F32)

        def multi(p, c):
            for u in range(MOE_TILES_PER_ITER):
                mm_tile(p * MOE_TILES_PER_ITER + u, MOE_TM)
            return c

        n_full = nt // (MOE_TM // MOE_UNIT)
        n_multi = n_full // MOE_TILES_PER_ITER
        lax.fori_loop(0, n_multi, multi, 0)

        def single(m, c):
            mm_tile(m, MOE_TM)
            return c

        lax.fori_loop(n_multi * MOE_TILES_PER_ITER, n_full, single, 0)

        @pl.when(nt > n_full * (MOE_TM // MOE_UNIT))
        def _():
            mm_tile(n_full, MOE_UNIT)

        @pl.when(j == n_steps - 1)
        def _():
            @pl.when(g > 0)
            def _():
                drain_scatter(nt_prev)

            def emit(m):
                slot = m & 1

                @pl.when(m >= 2)
                def _():
                    wait_tile_out(slot)

                _store_token_tiles(ystage, pl.multiple_of(slot * tm * pitch, tm * pitch), yacc[tile_rows(m), :],
                                   pitch)
                for_rows(m * tm, tm, lambda r: row_copy_out(r, slot).start())

            for_tiles(emit)


def _moe_ffn(xt, table, g_exp, g_tiles, g_max, w1, b1, w2, b2, layer, n_tokens):
    d = w2.shape[3]
    s = d // 128
    d_ff = w2.shape[2]
    n_steps = d_ff // MOE_TN
    tn = MOE_TN

    def jj(g, j, gt):
        return jnp.where(gt[g] > 0, j, n_steps - 1)

    grid_spec = pltpu.PrefetchScalarGridSpec(
        num_scalar_prefetch=2,
        grid=(g_max, n_steps),
        in_specs=[
            pl.BlockSpec(memory_space=pl.ANY),
            pl.BlockSpec(memory_space=pl.ANY),
            pl.BlockSpec((None, None, d, tn), lambda g, j, ge, gt: (layer, ge[g], 0, jj(g, j, gt))),
            pl.BlockSpec((None, None, d, tn), lambda g, j, ge, gt: (layer, ge[g], 0, jj(g, j, gt) + n_steps)),
            pl.BlockSpec((None, None, 1, tn), lambda g, j, ge, gt: (layer, ge[g], 0, jj(g, j, gt))),
            pl.BlockSpec((None, None, 1, tn), lambda g, j, ge, gt: (layer, ge[g], 0, jj(g, j, gt) + n_steps)),
            pl.BlockSpec((None, None, tn, d), lambda g, j, ge, gt: (layer, ge[g], jj(g, j, gt), 0)),
            pl.BlockSpec((None, None, 1, d), lambda g, j, ge, gt: (layer, ge[g], 0, 0)),
        ],
        out_specs=pl.BlockSpec(memory_space=pl.ANY),
        scratch_shapes=[
            pltpu.SMEM((2 * MOE_CAP,), I32),
            pltpu.VMEM((MOE_CAP * MOE_VMEM_PITCH, 128), F32),
            pltpu.VMEM((MOE_CAP, d), BF16),
            pltpu.VMEM((MOE_CAP, d), F32),
            pltpu.VMEM((2 * MOE_UNIT * MOE_VMEM_PITCH, 128), F32),
            pltpu.VMEM((d, tn), BF16),
            pltpu.VMEM((d, tn), BF16),
            pltpu.VMEM((tn, d), BF16),
            pltpu.SemaphoreType.DMA,
            pltpu.SemaphoreType.DMA,
            pltpu.SemaphoreType.DMA((2,)),
        ],
    )
    return pl.pallas_call(
        functools.partial(_moe_kernel, n_tokens=n_tokens, n_steps=n_steps),
        grid_spec=grid_spec,
        out_shape=jax.ShapeDtypeStruct(((TOP_K * n_tokens + MOE_UNIT) * s, 128), F32),
        compiler_params=_cparams(("arbitrary", "arbitrary")),
        name="moe_ffn",
    )(g_exp, g_tiles, table, xt, w1, w1, b1, b1, w2, b2)


def _combine_kernel(y0_ref, y1_ref, y2_ref, y3_ref, wt_ref, x_ref, g_ref, b_ref, o_ref, ob_ref, z_ref, *, alpha):
    tm, d = x_ref.shape
    s = d // 128
    wt = wt_ref[...]
    slots = [_load_token_tiles(y_ref, 0, tm, s) for y_ref in (y0_ref, y1_ref, y2_ref, y3_ref)]
    for a in range(s):
        cs = slice(a * 128, (a + 1) * 128)
        y = (wt[:, 0:1] * slots[0][a] + wt[:, 1:2] * slots[1][a]
             + wt[:, 2:3] * slots[2][a] + wt[:, 3:4] * slots[3][a])
        z_ref[:, cs] = alpha * x_ref[:, cs] + y
    z = _layer_norm_rows(z_ref[...], g_ref[...], b_ref[...])
    o_ref[...] = z
    ob_ref[...] = z.astype(BF16)


def _combine_ln(y4, wts, x, g, b, alpha, tm=256):
    t, d = x.shape
    s = d // 128
    nb = t // tm
    row = pl.BlockSpec((tm, d), lambda i: (i, 0))
    vec = pl.BlockSpec((1, d), lambda i: (0, 0))
    slot = [pl.BlockSpec((tm * s, 128), functools.partial(lambda i, k: (k * nb + i, 0), k=k))
            for k in range(TOP_K)]
    return pl.pallas_call(
        functools.partial(_combine_kernel, alpha=alpha),
        grid=(nb,),
        in_specs=slot + [pl.BlockSpec((tm, 128), lambda i: (i, 0)), row, vec, vec],
        out_specs=[row, row],
        out_shape=[jax.ShapeDtypeStruct((t, d), F32), jax.ShapeDtypeStruct((t, d), BF16)],
        scratch_shapes=[pltpu.VMEM((tm, d), F32)],
        compiler_params=_cparams(("arbitrary",)),
        name="combine_layernorm",
    )(y4, y4, y4, y4, wts, x, g, b)


def _pack_w_in(w_in_l):
    d = w_in_l.shape[0]
    att = ATT_HEADS * HEAD_DIM * 3
    n_f = ATT_HEADS
    conv = 3 * 4 * HEAD_DIM
    return jnp.concatenate([
        w_in_l[:, :att],
        w_in_l[:, att + n_f:att + n_f + conv],
        w_in_l[:, att:att + n_f],
        jnp.zeros((d, 512 - n_f), F32),
        w_in_l[:, att + n_f + conv:],
    ], axis=1)


def kernel(x, w_in, b_fgate, conv_w, lower_bounds, head_norm_g, w_out, ln1_g, ln1_b, w_router, b_router,
           w1, b1, w2, b2, ln2_g, ln2_b):
    batch, seq, d = x.shape
    depth = w_in.shape[0]
    t = batch * seq
    alpha = (2 * depth) ** 0.25

    lb = jnp.cumsum(jax.nn.softmax(lower_bounds.astype(F32), axis=0), axis=0)
    lb = lb - lb[0]
    b1r = b1.reshape(depth, N_EXPERTS, 1, b1.shape[-1])
    b2r = b2.reshape(depth, N_EXPERTS, 1, b2.shape[-1])

    xf = x.reshape(t, d)
    xb = xf.astype(BF16)
    for l in range(depth):
        proj = _matmul(xb, _pack_w_in(w_in[l]), 512, 1024, F32, "in_proj")
        bf_pad = jnp.zeros((1, 128), F32).at[0, :ATT_HEADS].set(b_fgate[l])
        ccol = _forget_cumsum(proj, bf_pad, batch, seq)
        qa, ka, vt = _att_prep(proj, ccol, batch, seq)
        o_att = _attention(qa, ka, vt, batch, seq)
        o_conv = _short_conv(proj, conv_w[l], batch, seq)
        o_rec = _hgrn2(proj, lb[l][None, :], batch, seq)
        o = _head_norm(o_att, o_conv, o_rec, proj, head_norm_g[l][None, :])
        mix = _matmul(o, w_out[l], 512, 1024, F32, "out_proj")
        x1, x1t = _add_ln(xf, mix, ln1_g[l][None, :], ln1_b[l][None, :], alpha)

        wr_pad = jnp.zeros((d, 128), F32).at[:, :N_EXPERTS].set(w_router[l])
        br_pad = jnp.zeros((1, 128), F32).at[0, :N_EXPERTS].set(b_router[l])
        idx, wts = _router(x1, wr_pad, br_pad)
        table, g_exp, g_tiles, g_max = _moe_plan(idx[:, :TOP_K], t)
        y4 = _moe_ffn(x1t, table, g_exp, g_tiles, g_max, w1, b1r, w2, b2r, l, t)
        xf, xb = _combine_ln(y4, wts, x1, ln2_g[l][None, :], ln2_b[l][None, :], alpha)
    return xf.reshape(batch, seq, d)
```

```python
import functools

import jax
import jax.numpy as jnp
import numpy as np
from jax import lax
from jax.experimental import pallas as pl
from jax.experimental.pallas import tpu as pltpu

F32 = jnp.float32
BF16 = jnp.bfloat16
I32 = jnp.int32
U32 = jnp.uint32

HEAD_DIM = 128
ATT_HEADS = 8
CONV_K = 3
HGRN_HEADS = 4
HGRN_CHUNK = 16
N_EXPERTS = 32
TOP_K = 4
SWIGLU_LIMIT = 7.0
SWIGLU_ALPHA = 1.702
LN_EPS = 1e-5
RMS_EPS = 1e-6
NEG_INF = float("-inf")
LOG2_E = 1.4426950408889634

V7X_VMEM_BYTES = 64 * 1024 * 1024
VMEM_LIMIT = 56 * 1024 * 1024

ATT_COLS = 3 * ATT_HEADS * HEAD_DIM
REST_COL0 = ATT_COLS + ATT_HEADS
COL512_CONV_B, COL512_CONV_C, COL512_CONV_H = 0, 1, 2
COL512_HQ, COL512_HF, COL512_HI, COL512_HG = 3, 4, 5, 6


def _cparams(sem):
    return pltpu.CompilerParams(dimension_semantics=sem, vmem_limit_bytes=VMEM_LIMIT)


def _log_sigmoid(z):
    return jnp.minimum(z, 0.0) - jnp.log1p(jnp.exp(-jnp.abs(z)))


def _mm_kernel(x_ref, w_ref, o_ref, wb_ref):
    @pl.when(pl.program_id(1) == 0)
    def _():
        wb_ref[...] = w_ref[...].astype(BF16)

    o_ref[...] = jnp.dot(x_ref[...].astype(BF16), wb_ref[...],
                         preferred_element_type=F32).astype(o_ref.dtype)


def _matmul(x, w, tm, tn, out_dtype, name, layer=None, n_cols=None):
    m, k = x.shape
    n = n_cols or w.shape[-1]
    if layer is None:
        w_spec = pl.BlockSpec((k, tn), lambda j, i: (0, j))
    else:
        w_spec = pl.BlockSpec((None, k, tn), lambda j, i: (layer, 0, j))
    return pl.pallas_call(
        _mm_kernel,
        grid=(n // tn, m // tm),
        in_specs=[pl.BlockSpec((tm, k), lambda j, i: (i, 0)), w_spec],
        out_specs=pl.BlockSpec((tm, tn), lambda j, i: (i, j)),
        out_shape=jax.ShapeDtypeStruct((m, n), out_dtype),
        scratch_shapes=[pltpu.VMEM((k, tn), BF16)],
        compiler_params=_cparams(("arbitrary", "arbitrary")),
        name=name,
    )(x, w)


def _fcum_kernel(f_ref, b_ref, ccol_ref, *, seq, blk):
    ls = _log_sigmoid(f_ref[...] + b_ref[...])
    r = lax.broadcasted_iota(I32, (blk, blk), 0)
    c = lax.broadcasted_iota(I32, (blk, blk), 1)
    tri = (r >= c).astype(F32)
    carry = jnp.zeros((1, 128), F32)
    for i in range(seq // blk):
        cs = jnp.dot(tri, ls[i * blk:(i + 1) * blk], precision=lax.Precision.HIGHEST,
                     preferred_element_type=F32) + carry
        ccol_ref[i * blk:(i + 1) * blk, :] = cs
        carry = cs[blk - 1:blk, :]


def _forget_cumsum(proj, b_fgate_pad, batch, seq):
    t = batch * seq
    return pl.pallas_call(
        functools.partial(_fcum_kernel, seq=seq, blk=256),
        grid=(batch,),
        in_specs=[pl.BlockSpec((seq, 128), lambda b: (b, 0)),
                  pl.BlockSpec((1, 128), lambda b: (0, 0))],
        out_specs=pl.BlockSpec((seq, 128), lambda b: (b, 0)),
        out_shape=jax.ShapeDtypeStruct((t, 128), F32),
        compiler_params=_cparams(("arbitrary",)),
        name="forget_cumsum",
    )(proj, b_fgate_pad)


ATT_AUG = 2 * HEAD_DIM


def _split3_bf16(c):
    hi = c.astype(BF16).astype(F32)
    r1 = c - hi
    mid = r1.astype(BF16).astype(F32)
    lo = (r1 - mid).astype(BF16).astype(F32)
    return hi, mid, lo


def _attprep_kernel(q_ref, k_ref, v_ref, cc_ref, qa_ref, ka_ref, vb_ref):
    scale = HEAD_DIM ** -0.5 * LOG2_E
    tm = q_ref.shape[0]
    lane = lax.broadcasted_iota(I32, (tm, HEAD_DIM), 1)
    vb_ref[...] = v_ref[...].T.astype(BF16)
    for h in range(ATT_HEADS):
        hs = slice(h * HEAD_DIM, (h + 1) * HEAD_DIM)
        c = jnp.broadcast_to(cc_ref[:, h:h + 1] * LOG2_E, (tm, HEAD_DIM))
        hi, mid, lo = _split3_bf16(c)
        one = jnp.ones_like(c)
        zero = jnp.zeros_like(c)
        q_aug = jnp.where(lane == 0, hi, jnp.where(lane == 1, mid, jnp.where(lane == 2, lo,
                          jnp.where(lane < 6, one, zero))))
        k_aug = jnp.where(lane < 3, one, jnp.where(lane == 3, -hi, jnp.where(lane == 4, -mid,
                          jnp.where(lane == 5, -lo, zero))))
        qa_ref[:, h * ATT_AUG:h * ATT_AUG + HEAD_DIM] = (q_ref[:, hs] * scale).astype(BF16)
        qa_ref[:, h * ATT_AUG + HEAD_DIM:(h + 1) * ATT_AUG] = q_aug.astype(BF16)
        ka_ref[:, h * ATT_AUG:h * ATT_AUG + HEAD_DIM] = k_ref[:, hs].astype(BF16)
        ka_ref[:, h * ATT_AUG + HEAD_DIM:(h + 1) * ATT_AUG] = k_aug.astype(BF16)


def _att_prep(proj, ccol, batch, seq, tm=512):
    t = proj.shape[0]
    ns = seq // tm
    width = ATT_HEADS * HEAD_DIM
    return pl.pallas_call(
        _attprep_kernel,
        grid=(batch, ns),
        in_specs=[pl.BlockSpec((tm, width), lambda b, i: (b * ns + i, 0)),
                  pl.BlockSpec((tm, width), lambda b, i: (b * ns + i, 1)),
                  pl.BlockSpec((tm, width), lambda b, i: (b * ns + i, 2)),
                  pl.BlockSpec((tm, 128), lambda b, i: (b * ns + i, 0))],
        out_specs=[pl.BlockSpec((tm, ATT_HEADS * ATT_AUG), lambda b, i: (b * ns + i, 0)),
                   pl.BlockSpec((tm, ATT_HEADS * ATT_AUG), lambda b, i: (b * ns + i, 0)),
                   pl.BlockSpec((None, width, tm), lambda b, i: (b, 0, i))],
        out_shape=[jax.ShapeDtypeStruct((t, ATT_HEADS * ATT_AUG), BF16),
                   jax.ShapeDtypeStruct((t, ATT_HEADS * ATT_AUG), BF16),
                   jax.ShapeDtypeStruct((batch, width, seq), BF16)],
        compiler_params=_cparams(("arbitrary", "arbitrary")),
        name="att_prep",
    )(proj, proj, proj, ccol)


def _att_kernel(qa_ref, ka_ref, vt_ref, o_ref, m_ref, l_ref, acc_ref, *, tq, tk, nk):
    i = pl.program_id(1)
    j = pl.program_id(2)

    @pl.when(j == 0)
    def _():
        m_ref[...] = jnp.full(m_ref.shape, NEG_INF, F32)
        l_ref[...] = jnp.zeros(l_ref.shape, F32)
        acc_ref[...] = jnp.zeros(acc_ref.shape, F32)

    def step(masked):
        if masked:
            mask = (lax.broadcasted_iota(I32, (tk, tq), 0) <= lax.broadcasted_iota(I32, (tk, tq), 1))
        def scores(h):
            as_ = slice(h * ATT_AUG, (h + 1) * ATT_AUG)
            s = lax.dot_general(ka_ref[:, as_], qa_ref[:, as_], (((1,), (1,)), ((), ())),
                                preferred_element_type=F32)
            return jnp.where(mask, s, NEG_INF) if masked else s

        s_next = scores(0)
        for h in range(ATT_HEADS):
            hs = slice(h * HEAD_DIM, (h + 1) * HEAD_DIM)
            s = s_next
            if h + 1 < ATT_HEADS:
                s_next = scores(h + 1)
            m_prev = m_ref[h:h + 1, :]
            m_new = jnp.maximum(m_prev, jnp.max(s, axis=0, keepdims=True))
            p = jnp.exp2(s - m_new)
            alpha = jnp.exp2(m_prev - m_new)
            l_ref[h:h + 1, :] = alpha * l_ref[h:h + 1, :] + jnp.sum(p, axis=0, keepdims=True)
            acc_ref[hs, :] = alpha * acc_ref[hs, :] + jnp.dot(vt_ref[hs, :], p.astype(BF16),
                                                            preferred_element_type=F32)
            m_ref[h:h + 1, :] = m_new

    @pl.when(j < i)
    def _():
        step(False)

    @pl.when(j == i)
    def _():
        step(True)

    @pl.when(j == nk - 1)
    def _():
        for h in range(ATT_HEADS):
            hs = slice(h * HEAD_DIM, (h + 1) * HEAD_DIM)
            o_ref[:, hs] = (acc_ref[hs, :] / l_ref[h:h + 1, :]).T


def _attention(qa, ka, vt, batch, seq, tq=512):
    t = batch * seq
    nq = seq // tq
    width = ATT_HEADS * HEAD_DIM
    kernel = functools.partial(_att_kernel, tq=tq, tk=tq, nk=nq)
    return pl.pallas_call(
        kernel,
        grid=(batch, nq, nq),
        in_specs=[pl.BlockSpec((tq, ATT_HEADS * ATT_AUG), lambda b, i, j: (b * nq + i, 0)),
                  pl.BlockSpec((tq, ATT_HEADS * ATT_AUG), lambda b, i, j: (b * nq + jnp.minimum(j, i), 0)),
                  pl.BlockSpec((None, width, tq), lambda b, i, j: (b, 0, jnp.minimum(j, i)))],
        out_specs=pl.BlockSpec((tq, width), lambda b, i, j: (b * nq + i, 0)),
        out_shape=jax.ShapeDtypeStruct((t, width), F32),
        scratch_shapes=[pltpu.VMEM((ATT_HEADS, tq), F32),
                        pltpu.VMEM((ATT_HEADS, tq), F32),
                        pltpu.VMEM((width, tq), F32)],
        compiler_params=_cparams(("arbitrary", "arbitrary", "arbitrary")),
        name="fox_attention",
    )(qa, ka, vt)


def _conv_kernel(b_ref, c_ref, h_ref, w_ref, o_ref, carry_ref, *, ts):
    @pl.when(pl.program_id(1) == 0)
    def _():
        carry_ref[...] = jnp.zeros(carry_ref.shape, F32)

    u = c_ref[...] * h_ref[...]
    ext = jnp.concatenate([carry_ref[...], u], axis=0)
    u1 = ext[7:7 + ts]
    u2 = ext[6:6 + ts]
    w = w_ref[...]
    y = w[2:3] * u + w[1:2] * u1 + w[0:1] * u2
    o_ref[...] = b_ref[...] * y
    carry_ref[...] = u[ts - 8:ts]


def _short_conv(proj, conv_w, batch, seq, ts=512):
    t = batch * seq
    ns = seq // ts
    w = conv_w.shape[1]
    return pl.pallas_call(
        functools.partial(_conv_kernel, ts=ts),
        grid=(batch, ns),
        in_specs=[pl.BlockSpec((ts, w), lambda b, i: (b * ns + i, COL512_CONV_B)),
                  pl.BlockSpec((ts, w), lambda b, i: (b * ns + i, COL512_CONV_C)),
                  pl.BlockSpec((ts, w), lambda b, i: (b * ns + i, COL512_CONV_H)),
                  pl.BlockSpec((CONV_K, w), lambda b, i: (0, 0))],
        out_specs=pl.BlockSpec((ts, w), lambda b, i: (b * ns + i, 0)),
        out_shape=jax.ShapeDtypeStruct((t, w), F32),
        scratch_shapes=[pltpu.VMEM((8, w), F32)],
        compiler_params=_cparams(("arbitrary", "arbitrary")),
        name="short_conv",
    )(proj, proj, proj, conv_w)


def _hgrn_kernel(q_ref, f_ref, i_ref, lb_ref, o_ref, st_ref, qs_ref, kk_ref, bc_ref, *, bs):
    c = HGRN_CHUNK

    @pl.when(pl.program_id(1) == 0)
    def _():
        st_ref[...] = jnp.zeros(st_ref.shape, F32)

    z = f_ref[...]
    lb = lb_ref[...]
    ls = _log_sigmoid(z)
    a = jnp.log(lb)
    b = jnp.log1p(-lb) + ls
    log_f = jnp.maximum(a, b) + jnp.log1p(jnp.exp(-jnp.abs(a - b)))
    kk_ref[...] = (1.0 - lb) * jnp.exp(ls - z)
    q = q_ref[...]
    qs_ref[...] = q / (1.0 + jnp.exp(-q))
    r = lax.broadcasted_iota(I32, (bs, bs), 0)
    cc = lax.broadcasted_iota(I32, (bs, bs), 1)
    tri = ((r // c == cc // c) & (r >= cc)).astype(F32)
    bc_ref[...] = jnp.dot(tri, log_f, precision=lax.Precision.HIGHEST, preferred_element_type=F32)

    t_idx = lax.broadcasted_iota(I32, (c, HEAD_DIM), 0)
    ones = jnp.ones((HEAD_DIM, HEAD_DIM), BF16)

    def chunk(n, carry):
        rows = pl.ds(pl.multiple_of(n * c, c), c)
        for h in range(HGRN_HEADS):
            hs = slice(h * HEAD_DIM, (h + 1) * HEAD_DIM)
            qh = qs_ref[rows, hs]
            kh = kk_ref[rows, hs]
            vh = i_ref[rows, hs]
            bh = bc_ref[rows, hs]
            b_last = bh[c - 1:c]
            parts = []
            for s in range(c):
                d = jnp.where(t_idx >= s, bh - bh[s:s + 1], NEG_INF)
                parts.append(qh * jnp.exp(d) * kh[s:s + 1])
            a3 = jnp.concatenate(parts, axis=0).astype(BF16)
            rs = jnp.dot(a3, ones, preferred_element_type=F32)
            o = lax.dot_general((qh * jnp.exp(bh)).astype(BF16), st_ref[h].astype(BF16),
                                (((1,), (1,)), ((), ())), preferred_element_type=F32)
            for s in range(c):
                o = o + rs[s * c:(s + 1) * c] * vh[s:s + 1]
            o_ref[rows, hs] = o
            kd = kh * jnp.exp(b_last - bh)
            kv_t = lax.dot_general(vh.astype(BF16), kd.astype(BF16), (((0,), (0,)), ((), ())),
                                   preferred_element_type=F32)
            st_ref[h] = jnp.exp(b_last) * st_ref[h] + kv_t
        return carry

    lax.fori_loop(0, bs // c, chunk, 0, unroll=4)


def _hgrn2(proj, lb, batch, seq, bs=256):
    t = batch * seq
    ns = seq // bs
    w = HGRN_HEADS * HEAD_DIM
    return pl.pallas_call(
        functools.partial(_hgrn_kernel, bs=bs),
        grid=(batch, ns),
        in_specs=[pl.BlockSpec((bs, w), lambda b, i: (b * ns + i, COL512_HQ)),
                  pl.BlockSpec((bs, w), lambda b, i: (b * ns + i, COL512_HF)),
                  pl.BlockSpec((bs, w), lambda b, i: (b * ns + i, COL512_HI)),
                  pl.BlockSpec((1, w), lambda b, i: (0, 0))],
        out_specs=pl.BlockSpec((bs, w), lambda b, i: (b * ns + i, 0)),
        out_shape=jax.ShapeDtypeStruct((t, w), F32),
        scratch_shapes=[pltpu.VMEM((HGRN_HEADS, HEAD_DIM, HEAD_DIM), F32),
                        pltpu.VMEM((bs, w), F32),
                        pltpu.VMEM((bs, w), F32),
                        pltpu.VMEM((bs, w), F32)],
        compiler_params=_cparams(("arbitrary", "arbitrary")),
        name="hgrn2",
    )(proj, proj, proj, lb)


def _headnorm_kernel(a_ref, c_ref, r_ref, hg_ref, g_ref, o_ref):
    na = a_ref.shape[1] // HEAD_DIM
    nc = c_ref.shape[1] // HEAD_DIM
    nr = r_ref.shape[1] // HEAD_DIM

    def rms(x, g):
        ms = jnp.mean(x * x, axis=-1, keepdims=True)
        return x * lax.rsqrt(ms + RMS_EPS) * g

    col = 0
    for src, n, gated in ((a_ref, na, False), (c_ref, nc, False), (r_ref, nr, True)):
        for h in range(n):
            hs = slice(h * HEAD_DIM, (h + 1) * HEAD_DIM)
            os_ = slice(col, col + HEAD_DIM)
            y = rms(src[:, hs], g_ref[:, os_])
            if gated:
                hg = hg_ref[:, hs]
                y = y * (hg / (1.0 + jnp.exp(-hg)))
            o_ref[:, os_] = y.astype(o_ref.dtype)
            col += HEAD_DIM


def _head_norm(o_att, o_conv, o_rec, proj, g, tm=256):
    t = o_att.shape[0]
    wa, wc, wr = o_att.shape[1], o_conv.shape[1], o_rec.shape[1]
    d = wa + wc + wr
    return pl.pallas_call(
        _headnorm_kernel,
        grid=(t // tm,),
        in_specs=[pl.BlockSpec((tm, wa), lambda i: (i, 0)),
                  pl.BlockSpec((tm, wc), lambda i: (i, 0)),
                  pl.BlockSpec((tm, wr), lambda i: (i, 0)),
                  pl.BlockSpec((tm, wr), lambda i: (i, COL512_HG)),
                  pl.BlockSpec((1, d), lambda i: (0, 0))],
        out_specs=pl.BlockSpec((tm, d), lambda i: (i, 0)),
        out_shape=jax.ShapeDtypeStruct((t, d), BF16),
        compiler_params=_cparams(("arbitrary",)),
        name="head_norm",
    )(o_att, o_conv, o_rec, proj, g)


def _store_token_tiles(ref, base, y, pitch=None):
    n, d = y.shape
    s = d // 128
    for a in range(s):
        ref[pl.ds(base + a, n, stride=pitch or s), :] = y[:, a * 128:(a + 1) * 128]


def _load_token_tiles(ref, base, n, s, pitch=None):
    return [ref[pl.ds(base + a, n, stride=pitch or s), :] for a in range(s)]


def _layer_norm_rows(z, g, b):
    mu = jnp.mean(z, axis=-1, keepdims=True)
    zc = z - mu
    var = jnp.mean(zc * zc, axis=-1, keepdims=True)
    return zc * lax.rsqrt(var + LN_EPS) * g + b


def _addln_kernel(x_ref, y_ref, g_ref, b_ref, o_ref, ot_ref, *, alpha):
    y = _layer_norm_rows(alpha * x_ref[...] + y_ref[...], g_ref[...], b_ref[...])
    o_ref[...] = y
    _store_token_tiles(ot_ref, 0, y)


def _add_ln(x, y, g, b, alpha, tm=256):
    t, d = x.shape
    s = d // 128
    row = pl.BlockSpec((tm, d), lambda i: (i, 0))
    vec = pl.BlockSpec((1, d), lambda i: (0, 0))
    return pl.pallas_call(
        functools.partial(_addln_kernel, alpha=alpha),
        grid=(t // tm,),
        in_specs=[row, row, vec, vec],
        out_specs=[row, pl.BlockSpec((tm * s, 128), lambda i: (i, 0))],
        out_shape=[jax.ShapeDtypeStruct((t, d), F32),
                   jax.ShapeDtypeStruct((t * s, 128), F32)],
        compiler_params=_cparams(("arbitrary",)),
        name="add_layernorm",
    )(x, y, g, b)


def _router_kernel(x_ref, w_ref, b_ref, idx_ref, wt_ref):
    logits = jnp.dot(x_ref[...], w_ref[...], precision=lax.Precision.HIGHEST,
                     preferred_element_type=F32) + b_ref[...]
    lane = lax.broadcasted_iota(I32, logits.shape, 1)
    logits = jnp.where(lane < N_EXPERTS, logits, NEG_INF)
    vals, idxs = [], []
    for _ in range(TOP_K):
        m = jnp.max(logits, axis=-1, keepdims=True)
        sel = jnp.min(jnp.where(logits == m, lane, 128), axis=-1, keepdims=True)
        vals.append(m)
        idxs.append(sel)
        logits = jnp.where(lane == sel, NEG_INF, logits)
    es = [jnp.exp(v - vals[0]) for v in vals]
    tot = es[0] + es[1] + es[2] + es[3]
    idx_out = jnp.zeros(logits.shape, I32)
    wt_out = jnp.zeros(logits.shape, F32)
    for k in range(TOP_K):
        idx_out = jnp.where(lane == k, idxs[k], idx_out)
        wt_out = jnp.where(lane == k, es[k] / tot, wt_out)
    idx_ref[...] = idx_out
    wt_ref[...] = wt_out


def _router(x, w_pad, b_pad, tm=512):
    t, d = x.shape
    return pl.pallas_call(
        _router_kernel,
        grid=(t // tm,),
        in_specs=[pl.BlockSpec((tm, d), lambda i: (i, 0)),
                  pl.BlockSpec((d, 128), lambda i: (0, 0)),
                  pl.BlockSpec((1, 128), lambda i: (0, 0))],
        out_specs=[pl.BlockSpec((tm, 128), lambda i: (i, 0)),
                   pl.BlockSpec((tm, 128), lambda i: (i, 0))],
        out_shape=[jax.ShapeDtypeStruct((t, 128), I32),
                   jax.ShapeDtypeStruct((t, 128), F32)],
        compiler_params=_cparams(("arbitrary",)),
        name="router_top4",
    )(x, w_pad, b_pad)


MOE_UNIT = 128
MOE_TM = 256
MOE_TILES_PER_ITER = 4
MOE_TN = 256
MOE_CAP = 1280
MOE_ROW_UNROLL = 8
MOE_VMEM_PITCH = 20
MOE_VMEM_PITCH_ALIGN = 4


def _moe_plan(top_idx, n_tokens):
    n_pairs = n_tokens * TOP_K
    g_max = N_EXPERTS + -(-n_pairs // MOE_CAP)
    e_flat = top_idx.reshape(-1)
    onehot = (e_flat[:, None] == jnp.arange(N_EXPERTS, dtype=I32)[None, :]).astype(I32)
    csum = jnp.cumsum(onehot, axis=0)
    rank = jnp.sum(csum * onehot, axis=1) - 1
    counts = csum[-1]
    n_grp = (counts + MOE_CAP - 1) // MOE_CAP
    g_end = jnp.cumsum(n_grp)
    g_base = g_end - n_grp
    pair = jnp.arange(n_pairs, dtype=I32)
    dst = (pair % TOP_K) * n_tokens + pair // TOP_K
    grp = g_base[e_flat] + rank // MOE_CAP
    row = rank % MOE_CAP
    trash = TOP_K * n_tokens + (jnp.arange(MOE_CAP, dtype=I32) % MOE_UNIT)
    table = jnp.broadcast_to(trash[None, :], (g_max, MOE_CAP))
    table = table.at[grp, row].set(dst)
    gid = jnp.arange(g_max, dtype=I32)
    n_used = g_end[-1]
    g_exp = jnp.sum((jnp.minimum(gid, n_used - 1)[:, None] >= g_end[None, :]).astype(I32), axis=1)
    g_rows = jnp.clip(counts[g_exp] - (gid - g_base[g_exp]) * MOE_CAP, 0, MOE_CAP)
    g_rows = jnp.where(gid < n_used, g_rows, 0)
    g_units = ((g_rows + MOE_UNIT - 1) // MOE_UNIT).astype(I32)
    return table, g_exp, g_units, g_max


def _moe_kernel(ge_ref, gt_ref, tbl_hbm, x_hbm, w1g_ref, w1u_ref, b1g_ref, b1u_ref, w2_ref, b2_ref,
                out_hbm, tbl_smem, xbuf, xb, yacc, ystage, w1g_b, w1u_b, w2_b, sem_tbl, sem_in, sem_out,
                *, n_tokens, n_steps):
    g = pl.program_id(0)
    j = pl.program_id(1)
    n_groups = pl.num_programs(0)
    nt = gt_ref[g]
    nt_prev = gt_ref[jnp.maximum(g - 1, 0)]
    g_next = jnp.minimum(g + 1, n_groups - 1)
    tm = MOE_UNIT
    d = yacc.shape[1]
    s = d // 128
    tbase = (g & 1) * MOE_CAP

    pitch = MOE_VMEM_PITCH

    def token_rows(r, n=1):
        return pl.ds(pl.multiple_of(r * s, s), n * s)

    def vmem_token(r):
        return pl.ds(pl.multiple_of(r * pitch, MOE_VMEM_PITCH_ALIGN), s)

    def start_gather(grp, n_tiles):
        base = (grp & 1) * MOE_CAP
        cp = pltpu.make_async_copy(tbl_hbm.at[grp], tbl_smem.at[pl.ds(base, MOE_CAP)], sem_tbl)
        cp.start()
        cp.wait()

        def row_copy_in(r):
            tok = tbl_smem[base + r] & (n_tokens - 1)
            pltpu.make_async_copy(x_hbm.at[token_rows(tok)], xbuf.at[vmem_token(r)], sem_in).start()

        for_rows(0, n_tiles * tm, row_copy_in)

    def row_copy_out(r, slot):
        src = ystage.at[vmem_token(slot * tm + (r & (tm - 1)))]
        return pltpu.make_async_copy(src, out_hbm.at[token_rows(tbl_smem[tbase + r])], sem_out.at[slot])

    def wait_tile_out(slot):
        pltpu.make_async_copy(ystage.at[token_rows(0, tm)], out_hbm.at[token_rows(0, tm)],
                              sem_out.at[slot]).wait()

    def drain_scatter(n_tiles):
        @pl.when(n_tiles >= 2)
        def _():
            wait_tile_out(n_tiles & 1)

        wait_tile_out((n_tiles - 1) & 1)

    def for_rows(first, n_rows, fn):
        def body(i, c):
            for u in range(MOE_ROW_UNROLL):
                fn(first + i * MOE_ROW_UNROLL + u)
            return c

        lax.fori_loop(0, n_rows // MOE_ROW_UNROLL, body, 0)

    def for_tiles(fn):
        def body(m, c):
            fn(m)
            return c

        lax.fori_loop(0, nt, body, 0)

    def tile_rows(m):
        return pl.ds(pl.multiple_of(m * tm, tm), tm)

    @pl.when((g == 0) & (j == 0))
    def _():
        ystage[pl.ds(0, tm * s), :] = jnp.zeros((tm * s, 128), F32)
        cp = pltpu.make_async_copy(ystage.at[token_rows(0, tm)],
                                   out_hbm.at[token_rows(TOP_K * n_tokens, tm)], sem_out.at[0])
        cp.start()
        cp.wait()
        start_gather(g, nt)

    @pl.when((nt == 0) & (j == 0) & (g > 0) & (nt_prev > 0))
    def _():
        drain_scatter(nt_prev)

    @pl.when(nt > 0)
    def _():
        @pl.when(j == 0)
        def _():
            def init_acc(m):
                yacc[tile_rows(m), :] = jnp.broadcast_to(b2_ref[...], (tm, d))

            for_tiles(init_acc)

            def wait_rows(m):
                pltpu.make_async_copy(x_hbm.at[token_rows(0, tm)], xbuf.at[token_rows(0, tm)], sem_in).wait()

            for_tiles(wait_rows)

            def to_rows(m):
                parts = _load_token_tiles(xbuf, pl.multiple_of(m * tm * pitch, tm * pitch), tm, s, pitch)
                for a in range(s):
                    xb[tile_rows(m), a * 128:(a + 1) * 128] = parts[a].astype(BF16)

            for_tiles(to_rows)

            @pl.when((g + 1 < n_groups) & (gt_ref[g_next] > 0))
            def _():
                start_gather(g_next, gt_ref[g_next])

        w1g_b[...] = w1g_ref[...].astype(BF16)
        w1u_b[...] = w1u_ref[...].astype(BF16)
        w2_b[...] = w2_ref[...].astype(BF16)

        def mm_tile(m, n_rows):
            rows = pl.ds(pl.multiple_of(m * MOE_TM, MOE_TM), n_rows)
            x = xb[rows, :]
            hg = jnp.dot(x, w1g_b[...], preferred_element_type=F32) + b1g_ref[...]
            hu = jnp.dot(x, w1u_b[...], preferred_element_type=F32) + b1u_ref[...]
            gate = jnp.minimum(hg, SWIGLU_LIMIT)
            up = jnp.clip(hu, -SWIGLU_LIMIT, SWIGLU_LIMIT)
            act = (up + 1.0) * gate / (1.0 + jnp.exp(-SWIGLU_ALPHA * gate))
            yacc[rows, :] += jnp.dot(act.astype(BF16), w2_b[...], preferred_element_type=F32)

        def multi(p, c):
            for u in range(MOE_TILES_PER_ITER):
                mm_tile(p * MOE_TILES_PER_ITER + u, MOE_TM)
            return c

        n_full = nt // (MOE_TM // MOE_UNIT)
        n_multi = n_full // MOE_TILES_PER_ITER
        lax.fori_loop(0, n_multi, multi, 0)

        def single(m, c):
            mm_tile(m, MOE_TM)
            return c

        lax.fori_loop(n_multi * MOE_TILES_PER_ITER, n_full, single, 0)

        @pl.when(nt > n_full * (MOE_TM // MOE_UNIT))
        def _():
            mm_tile(n_full, MOE_UNIT)

        @pl.when(j == n_steps - 1)
        def _():
            @pl.when(g > 0)
            def _():
                drain_scatter(nt_prev)

            def emit(m):
                slot = m & 1

                @pl.when(m >= 2)
                def _():
                    wait_tile_out(slot)

                _store_token_tiles(ystage, pl.multiple_of(slot * tm * pitch, tm * pitch), yacc[tile_rows(m), :],
                                   pitch)
                for_rows(m * tm, tm, lambda r: row_copy_out(r, slot).start())

            for_tiles(emit)


def _moe_ffn(xt, table, g_exp, g_units, g_max, w1, b1, w2, b2, layer, n_tokens):
    d = w2.shape[3]
    s = d // 128
    d_ff = w2.shape[2]
    n_steps = d_ff // MOE_TN
    tn = MOE_TN

    def jj(g, j, gt):
        return jnp.where(gt[g] > 0, j, n_steps - 1)

    grid_spec = pltpu.PrefetchScalarGridSpec(
        num_scalar_prefetch=2,
        grid=(g_max, n_steps),
        in_specs=[
            pl.BlockSpec(memory_space=pl.ANY),
            pl.BlockSpec(memory_space=pl.ANY),
            pl.BlockSpec((None, None, d, tn), lambda g, j, ge, gt: (layer, ge[g], 0, jj(g, j, gt))),
            pl.BlockSpec((None, None, d, tn), lambda g, j, ge, gt: (layer, ge[g], 0, jj(g, j, gt) + n_steps)),
            pl.BlockSpec((None, None, 1, tn), lambda g, j, ge, gt: (layer, ge[g], 0, jj(g, j, gt))),
            pl.BlockSpec((None, None, 1, tn), lambda g, j, ge, gt: (layer, ge[g], 0, jj(g, j, gt) + n_steps)),
            pl.BlockSpec((None, None, tn, d), lambda g, j, ge, gt: (layer, ge[g], jj(g, j, gt), 0)),
            pl.BlockSpec((None, None, 1, d), lambda g, j, ge, gt: (layer, ge[g], 0, 0)),
        ],
        out_specs=pl.BlockSpec(memory_space=pl.ANY),
        scratch_shapes=[
            pltpu.SMEM((2 * MOE_CAP,), I32),
            pltpu.VMEM((MOE_CAP * MOE_VMEM_PITCH, 128), F32),
            pltpu.VMEM((MOE_CAP, d), BF16),
            pltpu.VMEM((MOE_CAP, d), F32),
            pltpu.VMEM((2 * MOE_UNIT * MOE_VMEM_PITCH, 128), F32),
            pltpu.VMEM((d, tn), BF16),
            pltpu.VMEM((d, tn), BF16),
            pltpu.VMEM((tn, d), BF16),
            pltpu.SemaphoreType.DMA,
            pltpu.SemaphoreType.DMA,
            pltpu.SemaphoreType.DMA((2,)),
        ],
    )
    return pl.pallas_call(
        functools.partial(_moe_kernel, n_tokens=n_tokens, n_steps=n_steps),
        grid_spec=grid_spec,
        out_shape=jax.ShapeDtypeStruct(((TOP_K * n_tokens + MOE_UNIT) * s, 128), F32),
        compiler_params=_cparams(("arbitrary", "arbitrary")),
        name="moe_ffn",
    )(g_exp, g_units, table, xt, w1, w1, b1, b1, w2, b2)


def _combine_kernel(y0_ref, y1_ref, y2_ref, y3_ref, wt_ref, x_ref, g_ref, b_ref, o_ref, ob_ref, z_ref, *, alpha):
    tm, d = x_ref.shape
    s = d // 128
    wt = wt_ref[...]
    slots = [_load_token_tiles(y_ref, 0, tm, s) for y_ref in (y0_ref, y1_ref, y2_ref, y3_ref)]
    for a in range(s):
        cs = slice(a * 128, (a + 1) * 128)
        y = (wt[:, 0:1] * slots[0][a] + wt[:, 1:2] * slots[1][a]
             + wt[:, 2:3] * slots[2][a] + wt[:, 3:4] * slots[3][a])
        z_ref[:, cs] = alpha * x_ref[:, cs] + y
    z = _layer_norm_rows(z_ref[...], g_ref[...], b_ref[...])
    o_ref[...] = z
    ob_ref[...] = z.astype(BF16)


def _combine_ln(y4, wts, x, g, b, alpha, tm=256):
    t, d = x.shape
    s = d // 128
    nb = t // tm
    row = pl.BlockSpec((tm, d), lambda i: (i, 0))
    vec = pl.BlockSpec((1, d), lambda i: (0, 0))
    slot = [pl.BlockSpec((tm * s, 128), functools.partial(lambda i, k: (k * nb + i, 0), k=k))
            for k in range(TOP_K)]
    return pl.pallas_call(
        functools.partial(_combine_kernel, alpha=alpha),
        grid=(nb,),
        in_specs=slot + [pl.BlockSpec((tm, 128), lambda i: (i, 0)), row, vec, vec],
        out_specs=[row, row],
        out_shape=[jax.ShapeDtypeStruct((t, d), F32), jax.ShapeDtypeStruct((t, d), BF16)],
        scratch_shapes=[pltpu.VMEM((tm, d), F32)],
        compiler_params=_cparams(("arbitrary",)),
        name="combine_layernorm",
    )(y4, y4, y4, y4, wts, x, g, b)


def kernel(x, w_in, b_fgate, conv_w, lower_bounds, head_norm_g, w_out, ln1_g, ln1_b, w_router, b_router,
           w1, b1, w2, b2, ln2_g, ln2_b):
    batch, seq, d = x.shape
    depth = w_in.shape[0]
    t = batch * seq
    alpha = (2 * depth) ** 0.25

    lb = jnp.cumsum(jax.nn.softmax(lower_bounds.astype(F32), axis=0), axis=0)
    lb = lb - lb[0]
    b1r = b1.reshape(depth, N_EXPERTS, 1, b1.shape[-1])
    b2r = b2.reshape(depth, N_EXPERTS, 1, b2.shape[-1])

    xf = x.reshape(t, d)
    xb = xf.astype(BF16)
    for l in range(depth):
        w_f = jnp.zeros((d, 128), F32).at[:, :ATT_HEADS].set(w_in[l, :, ATT_COLS:REST_COL0])
        qkv = _matmul(xb, w_in, 512, 1024, F32, "in_proj_att", layer=l, n_cols=ATT_COLS)
        f_logit = _matmul(xb, w_f, 512, 128, F32, "in_proj_forget")
        proj = _matmul(xb, w_in[l, :, REST_COL0:], 512, 896, F32, "in_proj_rest")
        bf_pad = jnp.zeros((1, 128), F32).at[0, :ATT_HEADS].set(b_fgate[l])
        ccol = _forget_cumsum(f_logit, bf_pad, batch, seq)
        qa, ka, vt = _att_prep(qkv, ccol, batch, seq)
        o_att = _attention(qa, ka, vt, batch, seq)
        o_conv = _short_conv(proj, conv_w[l], batch, seq)
        o_rec = _hgrn2(proj, lb[l][None, :], batch, seq)
        o = _head_norm(o_att, o_conv, o_rec, proj, head_norm_g[l][None, :])
        mix = _matmul(o, w_out, 512, 1024, F32, "out_proj", layer=l)
        x1, x1t = _add_ln(xf, mix, ln1_g[l][None, :], ln1_b[l][None, :], alpha)

        wr_pad = jnp.zeros((d, 128), F32).at[:, :N_EXPERTS].set(w_router[l])
        br_pad = jnp.zeros((1, 128), F32).at[0, :N_EXPERTS].set(b_router[l])
        idx, wts = _router(x1, wr_pad, br_pad)
        table, g_exp, g_units, g_max = _moe_plan(idx[:, :TOP_K], t)
        y4 = _moe_ffn(x1t, table, g_exp, g_units, g_max, w1, b1r, w2, b2r, l, t)
        xf, xb = _combine_ln(y4, wts, x1, ln2_g[l][None, :], ln2_b[l][None, :], alpha)
    return xf.reshape(batch, seq, d)
```

```python
import functools

import jax
import jax.numpy as jnp
import numpy as np
from jax import lax
from jax.experimental import pallas as pl
from jax.experimental.pallas import tpu as pltpu

F32 = jnp.float32
BF16 = jnp.bfloat16
I32 = jnp.int32
U32 = jnp.uint32

HEAD_DIM = 128
ATT_HEADS = 8
CONV_K = 3
HGRN_HEADS = 4
HGRN_CHUNK = 16
N_EXPERTS = 32
TOP_K = 4
SWIGLU_LIMIT = 7.0
SWIGLU_ALPHA = 1.702
LN_EPS = 1e-5
RMS_EPS = 1e-6
NEG_INF = float("-inf")
LOG2_E = 1.4426950408889634

V7X_VMEM_BYTES = 64 * 1024 * 1024
VMEM_LIMIT = 56 * 1024 * 1024

ATT_COLS = 3 * ATT_HEADS * HEAD_DIM
REST_COL0 = ATT_COLS + ATT_HEADS
COL512_CONV_B, COL512_CONV_C, COL512_CONV_H = 0, 1, 2
COL512_HQ, COL512_HF, COL512_HI, COL512_HG = 3, 4, 5, 6


def _cparams(sem):
    return pltpu.CompilerParams(dimension_semantics=sem, vmem_limit_bytes=VMEM_LIMIT)


def _log_sigmoid(z):
    return jnp.minimum(z, 0.0) - jnp.log1p(jnp.exp(-jnp.abs(z)))


def _mm_kernel(x_ref, w_ref, o_ref, wb_ref):
    @pl.when(pl.program_id(1) == 0)
    def _():
        wb_ref[...] = w_ref[...].astype(BF16)

    o_ref[...] = jnp.dot(x_ref[...].astype(BF16), wb_ref[...],
                         preferred_element_type=F32).astype(o_ref.dtype)


def _matmul(x, w, tm, tn, out_dtype, name, layer=None, n_cols=None):
    m, k = x.shape
    n = n_cols or w.shape[-1]
    if layer is None:
        w_spec = pl.BlockSpec((k, tn), lambda j, i: (0, j))
    else:
        w_spec = pl.BlockSpec((None, k, tn), lambda j, i: (layer, 0, j))
    return pl.pallas_call(
        _mm_kernel,
        grid=(n // tn, m // tm),
        in_specs=[pl.BlockSpec((tm, k), lambda j, i: (i, 0)), w_spec],
        out_specs=pl.BlockSpec((tm, tn), lambda j, i: (i, j)),
        out_shape=jax.ShapeDtypeStruct((m, n), out_dtype),
        scratch_shapes=[pltpu.VMEM((k, tn), BF16)],
        compiler_params=_cparams(("arbitrary", "arbitrary")),
        name=name,
    )(x, w)


def _fcum_kernel(f_ref, b_ref, ccol_ref, *, seq, blk):
    ls = _log_sigmoid(f_ref[...] + b_ref[...])
    r = lax.broadcasted_iota(I32, (blk, blk), 0)
    c = lax.broadcasted_iota(I32, (blk, blk), 1)
    tri = (r >= c).astype(F32)
    carry = jnp.zeros((1, 128), F32)
    for i in range(seq // blk):
        cs = jnp.dot(tri, ls[i * blk:(i + 1) * blk], precision=lax.Precision.HIGHEST,
                     preferred_element_type=F32) + carry
        ccol_ref[i * blk:(i + 1) * blk, :] = cs
        carry = cs[blk - 1:blk, :]


def _forget_cumsum(proj, b_fgate_pad, batch, seq):
    t = batch * seq
    return pl.pallas_call(
        functools.partial(_fcum_kernel, seq=seq, blk=256),
        grid=(batch,),
        in_specs=[pl.BlockSpec((seq, 128), lambda b: (b, 0)),
                  pl.BlockSpec((1, 128), lambda b: (0, 0))],
        out_specs=pl.BlockSpec((seq, 128), lambda b: (b, 0)),
        out_shape=jax.ShapeDtypeStruct((t, 128), F32),
        compiler_params=_cparams(("arbitrary",)),
        name="forget_cumsum",
    )(proj, b_fgate_pad)


ATT_AUG = 2 * HEAD_DIM


def _split3_bf16(c):
    hi = c.astype(BF16).astype(F32)
    r1 = c - hi
    mid = r1.astype(BF16).astype(F32)
    lo = (r1 - mid).astype(BF16).astype(F32)
    return hi, mid, lo


def _attprep_kernel(q_ref, k_ref, v_ref, cc_ref, qa_ref, ka_ref, vb_ref):
    scale = HEAD_DIM ** -0.5 * LOG2_E
    tm = q_ref.shape[0]
    lane = lax.broadcasted_iota(I32, (tm, HEAD_DIM), 1)
    vb_ref[...] = v_ref[...].T.astype(BF16)
    for h in range(ATT_HEADS):
        hs = slice(h * HEAD_DIM, (h + 1) * HEAD_DIM)
        c = jnp.broadcast_to(cc_ref[:, h:h + 1] * LOG2_E, (tm, HEAD_DIM))
        hi, mid, lo = _split3_bf16(c)
        one = jnp.ones_like(c)
        zero = jnp.zeros_like(c)
        q_aug = jnp.where(lane == 0, hi, jnp.where(lane == 1, mid, jnp.where(lane == 2, lo,
                          jnp.where(lane < 6, one, zero))))
        k_aug = jnp.where(lane < 3, one, jnp.where(lane == 3, -hi, jnp.where(lane == 4, -mid,
                          jnp.where(lane == 5, -lo, zero))))
        qa_ref[:, h * ATT_AUG:h * ATT_AUG + HEAD_DIM] = (q_ref[:, hs] * scale).astype(BF16)
        qa_ref[:, h * ATT_AUG + HEAD_DIM:(h + 1) * ATT_AUG] = q_aug.astype(BF16)
        ka_ref[:, h * ATT_AUG:h * ATT_AUG + HEAD_DIM] = k_ref[:, hs].astype(BF16)
        ka_ref[:, h * ATT_AUG + HEAD_DIM:(h + 1) * ATT_AUG] = k_aug.astype(BF16)


def _att_prep(proj, ccol, batch, seq, tm=512):
    t = proj.shape[0]
    ns = seq // tm
    width = ATT_HEADS * HEAD_DIM
    return pl.pallas_call(
        _attprep_kernel,
        grid=(batch, ns),
        in_specs=[pl.BlockSpec((tm, width), lambda b, i: (b * ns + i, 0)),
                  pl.BlockSpec((tm, width), lambda b, i: (b * ns + i, 1)),
                  pl.BlockSpec((tm, width), lambda b, i: (b * ns + i, 2)),
                  pl.BlockSpec((tm, 128), lambda b, i: (b * ns + i, 0))],
        out_specs=[pl.BlockSpec((tm, ATT_HEADS * ATT_AUG), lambda b, i: (b * ns + i, 0)),
                   pl.BlockSpec((tm, ATT_HEADS * ATT_AUG), lambda b, i: (b * ns + i, 0)),
                   pl.BlockSpec((None, width, tm), lambda b, i: (b, 0, i))],
        out_shape=[jax.ShapeDtypeStruct((t, ATT_HEADS * ATT_AUG), BF16),
                   jax.ShapeDtypeStruct((t, ATT_HEADS * ATT_AUG), BF16),
                   jax.ShapeDtypeStruct((batch, width, seq), BF16)],
        compiler_params=_cparams(("arbitrary", "arbitrary")),
        name="att_prep",
    )(proj, proj, proj, ccol)


def _att_kernel(qa_ref, ka_ref, vt_ref, o_ref, m_ref, l_ref, acc_ref, *, tq, tk, nk):
    i = pl.program_id(1)
    j = pl.program_id(2)

    @pl.when(j == 0)
    def _():
        m_ref[...] = jnp.full(m_ref.shape, NEG_INF, F32)
        l_ref[...] = jnp.zeros(l_ref.shape, F32)
        acc_ref[...] = jnp.zeros(acc_ref.shape, F32)

    def step(masked):
        if masked:
            mask = (lax.broadcasted_iota(I32, (tk, tq), 0) <= lax.broadcasted_iota(I32, (tk, tq), 1))
        def scores(h):
            as_ = slice(h * ATT_AUG, (h + 1) * ATT_AUG)
            s = lax.dot_general(ka_ref[:, as_], qa_ref[:, as_], (((1,), (1,)), ((), ())),
                                preferred_element_type=F32)
            return jnp.where(mask, s, NEG_INF) if masked else s

        s_next = scores(0)
        for h in range(ATT_HEADS):
            hs = slice(h * HEAD_DIM, (h + 1) * HEAD_DIM)
            s = s_next
            if h + 1 < ATT_HEADS:
                s_next = scores(h + 1)
            m_prev = m_ref[h:h + 1, :]
            m_new = jnp.maximum(m_prev, jnp.max(s, axis=0, keepdims=True))
            p = jnp.exp2(s - m_new)
            alpha = jnp.exp2(m_prev - m_new)
            l_ref[h:h + 1, :] = alpha * l_ref[h:h + 1, :] + jnp.sum(p, axis=0, keepdims=True)
            acc_ref[hs, :] = alpha * acc_ref[hs, :] + jnp.dot(vt_ref[hs, :], p.astype(BF16),
                                                            preferred_element_type=F32)
            m_ref[h:h + 1, :] = m_new

    @pl.when(j < i)
    def _():
        step(False)

    @pl.when(j == i)
    def _():
        step(True)

    @pl.when(j == nk - 1)
    def _():
        for h in range(ATT_HEADS):
            hs = slice(h * HEAD_DIM, (h + 1) * HEAD_DIM)
            o_ref[:, hs] = (acc_ref[hs, :] / l_ref[h:h + 1, :]).T


def _attention(qa, ka, vt, batch, seq, tq=512):
    t = batch * seq
    nq = seq // tq
    width = ATT_HEADS * HEAD_DIM
    kernel = functools.partial(_att_kernel, tq=tq, tk=tq, nk=nq)
    return pl.pallas_call(
        kernel,
        grid=(batch, nq, nq),
        in_specs=[pl.BlockSpec((tq, ATT_HEADS * ATT_AUG), lambda b, i, j: (b * nq + i, 0)),
                  pl.BlockSpec((tq, ATT_HEADS * ATT_AUG), lambda b, i, j: (b * nq + jnp.minimum(j, i), 0)),
                  pl.BlockSpec((None, width, tq), lambda b, i, j: (b, 0, jnp.minimum(j, i)))],
        out_specs=pl.BlockSpec((tq, width), lambda b, i, j: (b * nq + i, 0)),
        out_shape=jax.ShapeDtypeStruct((t, width), F32),
        scratch_shapes=[pltpu.VMEM((ATT_HEADS, tq), F32),
                        pltpu.VMEM((ATT_HEADS, tq), F32),
                        pltpu.VMEM((width, tq), F32)],
        compiler_params=_cparams(("arbitrary", "arbitrary", "arbitrary")),
        name="fox_attention",
    )(qa, ka, vt)


def _conv_kernel(b_ref, c_ref, h_ref, w_ref, o_ref, carry_ref, *, ts):
    @pl.when(pl.program_id(1) == 0)
    def _():
        carry_ref[...] = jnp.zeros(carry_ref.shape, F32)

    u = c_ref[...] * h_ref[...]
    ext = jnp.concatenate([carry_ref[...], u], axis=0)
    u1 = ext[7:7 + ts]
    u2 = ext[6:6 + ts]
    w = w_ref[...]
    y = w[2:3] * u + w[1:2] * u1 + w[0:1] * u2
    o_ref[...] = b_ref[...] * y
    carry_ref[...] = u[ts - 8:ts]


def _short_conv(proj, conv_w, batch, seq, ts=512):
    t = batch * seq
    ns = seq // ts
    w = conv_w.shape[1]
    return pl.pallas_call(
        functools.partial(_conv_kernel, ts=ts),
        grid=(batch, ns),
        in_specs=[pl.BlockSpec((ts, w), lambda b, i: (b * ns + i, COL512_CONV_B)),
                  pl.BlockSpec((ts, w), lambda b, i: (b * ns + i, COL512_CONV_C)),
                  pl.BlockSpec((ts, w), lambda b, i: (b * ns + i, COL512_CONV_H)),
                  pl.BlockSpec((CONV_K, w), lambda b, i: (0, 0))],
        out_specs=pl.BlockSpec((ts, w), lambda b, i: (b * ns + i, 0)),
        out_shape=jax.ShapeDtypeStruct((t, w), F32),
        scratch_shapes=[pltpu.VMEM((8, w), F32)],
        compiler_params=_cparams(("arbitrary", "arbitrary")),
        name="short_conv",
    )(proj, proj, proj, conv_w)


def _hgrn_kernel(q_ref, f_ref, i_ref, lb_ref, o_ref, st_ref, qs_ref, kk_ref, bc_ref, *, bs):
    c = HGRN_CHUNK

    @pl.when(pl.program_id(1) == 0)
    def _():
        st_ref[...] = jnp.zeros(st_ref.shape, F32)

    z = f_ref[...]
    lb = lb_ref[...]
    ls = _log_sigmoid(z)
    a = jnp.log(lb)
    b = jnp.log1p(-lb) + ls
    log_f = jnp.maximum(a, b) + jnp.log1p(jnp.exp(-jnp.abs(a - b)))
    kk_ref[...] = (1.0 - lb) * jnp.exp(ls - z)
    q = q_ref[...]
    qs_ref[...] = q / (1.0 + jnp.exp(-q))
    r = lax.broadcasted_iota(I32, (bs, bs), 0)
    cc = lax.broadcasted_iota(I32, (bs, bs), 1)
    tri = ((r // c == cc // c) & (r >= cc)).astype(F32)
    bc_ref[...] = jnp.dot(tri, log_f, precision=lax.Precision.HIGHEST, preferred_element_type=F32)

    t_idx = lax.broadcasted_iota(I32, (c, HEAD_DIM), 0)
    ones = jnp.ones((HEAD_DIM, HEAD_DIM), BF16)

    def chunk(n, carry):
        rows = pl.ds(pl.multiple_of(n * c, c), c)
        for h in range(HGRN_HEADS):
            hs = slice(h * HEAD_DIM, (h + 1) * HEAD_DIM)
            qh = qs_ref[rows, hs]
            kh = kk_ref[rows, hs]
            vh = i_ref[rows, hs]
            bh = bc_ref[rows, hs]
            b_last = bh[c - 1:c]
            parts = []
            for s in range(c):
                d = jnp.where(t_idx >= s, bh - bh[s:s + 1], NEG_INF)
                parts.append(qh * jnp.exp(d) * kh[s:s + 1])
            a3 = jnp.concatenate(parts, axis=0).astype(BF16)
            rs = jnp.dot(a3, ones, preferred_element_type=F32)
            o = lax.dot_general((qh * jnp.exp(bh)).astype(BF16), st_ref[h].astype(BF16),
                                (((1,), (1,)), ((), ())), preferred_element_type=F32)
            for s in range(c):
                o = o + rs[s * c:(s + 1) * c] * vh[s:s + 1]
            o_ref[rows, hs] = o
            kd = kh * jnp.exp(b_last - bh)
            kv_t = lax.dot_general(vh.astype(BF16), kd.astype(BF16), (((0,), (0,)), ((), ())),
                                   preferred_element_type=F32)
            st_ref[h] = jnp.exp(b_last) * st_ref[h] + kv_t
        return carry

    lax.fori_loop(0, bs // c, chunk, 0, unroll=4)


def _hgrn2(proj, lb, batch, seq, bs=256):
    t = batch * seq
    ns = seq // bs
    w = HGRN_HEADS * HEAD_DIM
    return pl.pallas_call(
        functools.partial(_hgrn_kernel, bs=bs),
        grid=(batch, ns),
        in_specs=[pl.BlockSpec((bs, w), lambda b, i: (b * ns + i, COL512_HQ)),
                  pl.BlockSpec((bs, w), lambda b, i: (b * ns + i, COL512_HF)),
                  pl.BlockSpec((bs, w), lambda b, i: (b * ns + i, COL512_HI)),
                  pl.BlockSpec((1, w), lambda b, i: (0, 0))],
        out_specs=pl.BlockSpec((bs, w), lambda b, i: (b * ns + i, 0)),
        out_shape=jax.ShapeDtypeStruct((t, w), F32),
        scratch_shapes=[pltpu.VMEM((HGRN_HEADS, HEAD_DIM, HEAD_DIM), F32),
                        pltpu.VMEM((bs, w), F32),
                        pltpu.VMEM((bs, w), F32),
                        pltpu.VMEM((bs, w), F32)],
        compiler_params=_cparams(("arbitrary", "arbitrary")),
        name="hgrn2",
    )(proj, proj, proj, lb)


def _headnorm_kernel(a_ref, c_ref, r_ref, hg_ref, g_ref, o_ref):
    na = a_ref.shape[1] // HEAD_DIM
    nc = c_ref.shape[1] // HEAD_DIM
    nr = r_ref.shape[1] // HEAD_DIM

    def rms(x, g):
        ms = jnp.mean(x * x, axis=-1, keepdims=True)
        return x * lax.rsqrt(ms + RMS_EPS) * g

    col = 0
    for src, n, gated in ((a_ref, na, False), (c_ref, nc, False), (r_ref, nr, True)):
        for h in range(n):
            hs = slice(h * HEAD_DIM, (h + 1) * HEAD_DIM)
            os_ = slice(col, col + HEAD_DIM)
            y = rms(src[:, hs], g_ref[:, os_])
            if gated:
                hg = hg_ref[:, hs]
                y = y * (hg / (1.0 + jnp.exp(-hg)))
            o_ref[:, os_] = y.astype(o_ref.dtype)
            col += HEAD_DIM


def _head_norm(o_att, o_conv, o_rec, proj, g, tm=256):
    t = o_att.shape[0]
    wa, wc, wr = o_att.shape[1], o_conv.shape[1], o_rec.shape[1]
    d = wa + wc + wr
    return pl.pallas_call(
        _headnorm_kernel,
        grid=(t // tm,),
        in_specs=[pl.BlockSpec((tm, wa), lambda i: (i, 0)),
                  pl.BlockSpec((tm, wc), lambda i: (i, 0)),
                  pl.BlockSpec((tm, wr), lambda i: (i, 0)),
                  pl.BlockSpec((tm, wr), lambda i: (i, COL512_HG)),
                  pl.BlockSpec((1, d), lambda i: (0, 0))],
        out_specs=pl.BlockSpec((tm, d), lambda i: (i, 0)),
        out_shape=jax.ShapeDtypeStruct((t, d), BF16),
        compiler_params=_cparams(("arbitrary",)),
        name="head_norm",
    )(o_att, o_conv, o_rec, proj, g)


def _store_token_tiles(ref, base, y, pitch=None):
    n, d = y.shape
    s = d // 128
    for a in range(s):
        ref[pl.ds(base + a, n, stride=pitch or s), :] = y[:, a * 128:(a + 1) * 128]


def _load_token_tiles(ref, base, n, s, pitch=None):
    return [ref[pl.ds(base + a, n, stride=pitch or s), :] for a in range(s)]


def _layer_norm_rows(z, g, b):
    mu = jnp.mean(z, axis=-1, keepdims=True)
    zc = z - mu
    var = jnp.mean(zc * zc, axis=-1, keepdims=True)
    return zc * lax.rsqrt(var + LN_EPS) * g + b


def _addln_kernel(x_ref, y_ref, g_ref, b_ref, o_ref, ot_ref, *, alpha):
    y = _layer_norm_rows(alpha * x_ref[...] + y_ref[...], g_ref[...], b_ref[...])
    o_ref[...] = y
    _store_token_tiles(ot_ref, 0, y)


def _add_ln(x, y, g, b, alpha, tm=256):
    t, d = x.shape
    s = d // 128
    row = pl.BlockSpec((tm, d), lambda i: (i, 0))
    vec = pl.BlockSpec((1, d), lambda i: (0, 0))
    return pl.pallas_call(
        functools.partial(_addln_kernel, alpha=alpha),
        grid=(t // tm,),
        in_specs=[row, row, vec, vec],
        out_specs=[row, pl.BlockSpec((tm * s, 128), lambda i: (i, 0))],
        out_shape=[jax.ShapeDtypeStruct((t, d), F32),
                   jax.ShapeDtypeStruct((t * s, 128), F32)],
        compiler_params=_cparams(("arbitrary",)),
        name="add_layernorm",
    )(x, y, g, b)


def _router_kernel(x_ref, w_ref, b_ref, idx_ref, wt_ref):
    logits = jnp.dot(x_ref[...], w_ref[...], precision=lax.Precision.HIGHEST,
                     preferred_element_type=F32) + b_ref[...]
    lane = lax.broadcasted_iota(I32, logits.shape, 1)
    logits = jnp.where(lane < N_EXPERTS, logits, NEG_INF)
    vals, idxs = [], []
    for _ in range(TOP_K):
        m = jnp.max(logits, axis=-1, keepdims=True)
        sel = jnp.min(jnp.where(logits == m, lane, 128), axis=-1, keepdims=True)
        vals.append(m)
        idxs.append(sel)
        logits = jnp.where(lane == sel, NEG_INF, logits)
    es = [jnp.exp(v - vals[0]) for v in vals]
    tot = es[0] + es[1] + es[2] + es[3]
    idx_out = jnp.zeros(logits.shape, I32)
    wt_out = jnp.zeros(logits.shape, F32)
    for k in range(TOP_K):
        idx_out = jnp.where(lane == k, idxs[k], idx_out)
        wt_out = jnp.where(lane == k, es[k] / tot, wt_out)
    idx_ref[...] = idx_out
    wt_ref[...] = wt_out


def _router(x, w_pad, b_pad, tm=512):
    t, d = x.shape
    return pl.pallas_call(
        _router_kernel,
        grid=(t // tm,),
        in_specs=[pl.BlockSpec((tm, d), lambda i: (i, 0)),
                  pl.BlockSpec((d, 128), lambda i: (0, 0)),
                  pl.BlockSpec((1, 128), lambda i: (0, 0))],
        out_specs=[pl.BlockSpec((tm, 128), lambda i: (i, 0)),
                   pl.BlockSpec((tm, 128), lambda i: (i, 0))],
        out_shape=[jax.ShapeDtypeStruct((t, 128), I32),
                   jax.ShapeDtypeStruct((t, 128), F32)],
        compiler_params=_cparams(("arbitrary",)),
        name="router_top4",
    )(x, w_pad, b_pad)


MOE_UNIT = 128
MOE_TM = 256
MOE_TILES_PER_ITER = 4
MOE_TN = 256
MOE_CAP = 1280
MOE_ROW_UNROLL = 8
MOE_VMEM_PITCH = 20
MOE_VMEM_PITCH_ALIGN = 4


MOE_PLAN_GROUPS = 128
MOE_CAP_SHIFT = 26
MOE_CAP_RECIP = -(-(1 << MOE_CAP_SHIFT) // MOE_CAP)


def _plan_kernel(idx_ref, gb_ref, tbl_ref, run_ref, hi_ref, lo_ref, cnt_ref, *, n_tokens):
    i = pl.program_id(0)
    tm = idx_ref.shape[0]

    @pl.when(i == 0)
    def _():
        run_ref[...] = jnp.zeros(run_ref.shape, F32)
        hi_ref[...] = jnp.zeros(hi_ref.shape, F32)
        lo_ref[...] = jnp.zeros(lo_ref.shape, F32)
        cnt_ref[...] = jnp.zeros(cnt_ref.shape, F32)

    idx = idx_ref[...]
    lane = lax.broadcasted_iota(I32, (tm, 128), 1)
    tri = (lax.broadcasted_iota(I32, (tm, tm), 0) >= lax.broadcasted_iota(I32, (tm, tm), 1)).astype(BF16)
    g_base = gb_ref[...].astype(F32)
    run = run_ref[...]
    tok = i * tm + lax.broadcasted_iota(I32, (tm, 1), 0)
    col_g = lax.broadcasted_iota(I32, (tm, MOE_PLAN_GROUPS), 1)
    col_r = lax.broadcasted_iota(I32, (tm, MOE_CAP), 1)
    tn = (((0,), (0,)), ((), ()))
    for k in range(TOP_K):
        onehot = (lane == idx[:, k:k + 1]).astype(F32)
        prefix = jnp.dot(tri, onehot.astype(BF16), preferred_element_type=F32)
        rank = jnp.sum(onehot * (run + prefix - 1.0), axis=-1, keepdims=True).astype(I32)
        first = jnp.sum(onehot * g_base, axis=-1, keepdims=True).astype(I32)
        run = run + prefix[tm - 1:tm, :]
        sub = (rank * MOE_CAP_RECIP) >> MOE_CAP_SHIFT
        grp = first + sub
        row = rank - sub * MOE_CAP
        dst = k * n_tokens + tok
        sel_g = (col_g == grp).astype(BF16)
        sel_r = col_r == row
        d_hi = (dst >> 8).astype(F32)
        d_lo = (dst & 255).astype(F32)
        cnt_ref[...] += lax.dot_general(sel_g, sel_r.astype(BF16), tn, preferred_element_type=F32)
        hi_ref[...] += lax.dot_general(sel_g, jnp.where(sel_r, d_hi, 0.0).astype(BF16), tn,
                                       preferred_element_type=F32)
        lo_ref[...] += lax.dot_general(sel_g, jnp.where(sel_r, d_lo, 0.0).astype(BF16), tn,
                                       preferred_element_type=F32)
    run_ref[...] = run

    @pl.when(i == pl.num_programs(0) - 1)
    def _():
        trash = TOP_K * n_tokens + (lax.broadcasted_iota(I32, tbl_ref.shape, 1) & (MOE_UNIT - 1))
        val = hi_ref[...].astype(I32) * 256 + lo_ref[...].astype(I32)
        tbl_ref[...] = jnp.where(cnt_ref[...] > 0.5, val, trash)


def _moe_plan(idx, n_tokens, tm=512):
    n_pairs = n_tokens * TOP_K
    g_max = N_EXPERTS + -(-n_pairs // MOE_CAP)
    assert g_max <= MOE_PLAN_GROUPS and n_pairs <= (1 << 15) and TOP_K * n_tokens + MOE_UNIT <= (1 << 16)
    assert all((r * MOE_CAP_RECIP) >> MOE_CAP_SHIFT == r // MOE_CAP for r in range(1 << 15))
    counts = jnp.sum((idx[:, :TOP_K, None] == jnp.arange(N_EXPERTS, dtype=I32)).astype(I32), axis=(0, 1))
    n_grp = (counts + MOE_CAP - 1) // MOE_CAP
    g_end = jnp.cumsum(n_grp)
    g_base = g_end - n_grp
    gb_pad = jnp.zeros((1, 128), I32).at[0, :N_EXPERTS].set(g_base)
    acc = pltpu.VMEM((MOE_PLAN_GROUPS, MOE_CAP), F32)
    table = pl.pallas_call(
        functools.partial(_plan_kernel, n_tokens=n_tokens),
        grid=(n_tokens // tm,),
        in_specs=[pl.BlockSpec((tm, 128), lambda i: (i, 0)),
                  pl.BlockSpec((1, 128), lambda i: (0, 0))],
        out_specs=pl.BlockSpec((MOE_PLAN_GROUPS, MOE_CAP), lambda i: (0, 0)),
        out_shape=jax.ShapeDtypeStruct((MOE_PLAN_GROUPS, MOE_CAP), I32),
        scratch_shapes=[pltpu.VMEM((1, 128), F32), acc, acc, acc],
        compiler_params=_cparams(("arbitrary",)),
        name="moe_plan",
    )(idx, gb_pad)
    gid = jnp.arange(g_max, dtype=I32)
    n_used = g_end[-1]
    g_exp = jnp.sum((jnp.minimum(gid, n_used - 1)[:, None] >= g_end[None, :]).astype(I32), axis=1)
    g_rows = jnp.clip(counts[g_exp] - (gid - g_base[g_exp]) * MOE_CAP, 0, MOE_CAP)
    g_rows = jnp.where(gid < n_used, g_rows, 0)
    g_units = ((g_rows + MOE_UNIT - 1) // MOE_UNIT).astype(I32)
    return table, g_exp, g_units, g_max


def _moe_kernel(ge_ref, gt_ref, tbl_hbm, x_hbm, w1g_ref, w1u_ref, b1g_ref, b1u_ref, w2_ref, b2_ref,
                out_hbm, tbl_smem, xbuf, xb, yacc, ystage, w1g_b, w1u_b, w2_b, sem_tbl, sem_in, sem_out,
                *, n_tokens, n_steps):
    g = pl.program_id(0)
    j = pl.program_id(1)
    n_groups = pl.num_programs(0)
    nt = gt_ref[g]
    nt_prev = gt_ref[jnp.maximum(g - 1, 0)]
    g_next = jnp.minimum(g + 1, n_groups - 1)
    tm = MOE_UNIT
    d = yacc.shape[1]
    s = d // 128
    tbase = (g & 1) * MOE_CAP

    pitch = MOE_VMEM_PITCH

    def token_rows(r, n=1):
        return pl.ds(pl.multiple_of(r * s, s), n * s)

    def vmem_token(r):
        return pl.ds(pl.multiple_of(r * pitch, MOE_VMEM_PITCH_ALIGN), s)

    def start_gather(grp, n_tiles):
        base = (grp & 1) * MOE_CAP
        cp = pltpu.make_async_copy(tbl_hbm.at[grp], tbl_smem.at[pl.ds(base, MOE_CAP)], sem_tbl)
        cp.start()
        cp.wait()

        def row_copy_in(r):
            tok = tbl_smem[base + r] & (n_tokens - 1)
            pltpu.make_async_copy(x_hbm.at[token_rows(tok)], xbuf.at[vmem_token(r)], sem_in).start()

        for_rows(0, n_tiles * tm, row_copy_in)

    def row_copy_out(r, slot):
        src = ystage.at[vmem_token(slot * tm + (r & (tm - 1)))]
        return pltpu.make_async_copy(src, out_hbm.at[token_rows(tbl_smem[tbase + r])], sem_out.at[slot])

    def wait_tile_out(slot):
        pltpu.make_async_copy(ystage.at[token_rows(0, tm)], out_hbm.at[token_rows(0, tm)],
                              sem_out.at[slot]).wait()

    def drain_scatter(n_tiles):
        @pl.when(n_tiles >= 2)
        def _():
            wait_tile_out(n_tiles & 1)

        wait_tile_out((n_tiles - 1) & 1)

    def for_rows(first, n_rows, fn):
        def body(i, c):
            for u in range(MOE_ROW_UNROLL):
                fn(first + i * MOE_ROW_UNROLL + u)
            return c

        lax.fori_loop(0, n_rows // MOE_ROW_UNROLL, body, 0)

    def for_tiles(fn):
        def body(m, c):
            fn(m)
            return c

        lax.fori_loop(0, nt, body, 0)

    def tile_rows(m):
        return pl.ds(pl.multiple_of(m * tm, tm), tm)

    @pl.when((g == 0) & (j == 0))
    def _():
        ystage[pl.ds(0, tm * s), :] = jnp.zeros((tm * s, 128), F32)
        cp = pltpu.make_async_copy(ystage.at[token_rows(0, tm)],
                                   out_hbm.at[token_rows(TOP_K * n_tokens, tm)], sem_out.at[0])
        cp.start()
        cp.wait()
        start_gather(g, nt)

    @pl.when((nt == 0) & (j == 0) & (g > 0) & (nt_prev > 0))
    def _():
        drain_scatter(nt_prev)

    @pl.when(nt > 0)
    def _():
        @pl.when(j == 0)
        def _():
            def init_acc(m):
                yacc[tile_rows(m), :] = jnp.broadcast_to(b2_ref[...], (tm, d))

            for_tiles(init_acc)

            def wait_rows(m):
                pltpu.make_async_copy(x_hbm.at[token_rows(0, tm)], xbuf.at[token_rows(0, tm)], sem_in).wait()

            for_tiles(wait_rows)

            def to_rows(m):
                parts = _load_token_tiles(xbuf, pl.multiple_of(m * tm * pitch, tm * pitch), tm, s, pitch)
                for a in range(s):
                    xb[tile_rows(m), a * 128:(a + 1) * 128] = parts[a].astype(BF16)

            for_tiles(to_rows)

            @pl.when((g + 1 < n_groups) & (gt_ref[g_next] > 0))
            def _():
                start_gather(g_next, gt_ref[g_next])

        w1g_b[...] = w1g_ref[...].astype(BF16)
        w1u_b[...] = w1u_ref[...].astype(BF16)
        w2_b[...] = w2_ref[...].astype(BF16)

        def mm_tile(m, n_rows):
            rows = pl.ds(pl.multiple_of(m * MOE_TM, MOE_TM), n_rows)
            x = xb[rows, :]
            hg = jnp.dot(x, w1g_b[...], preferred_element_type=F32) + b1g_ref[...]
            hu = jnp.dot(x, w1u_b[...], preferred_element_type=F32) + b1u_ref[...]
            gate = jnp.minimum(hg, SWIGLU_LIMIT)
            up = jnp.clip(hu, -SWIGLU_LIMIT, SWIGLU_LIMIT)
            act = (up + 1.0) * gate / (1.0 + jnp.exp(-SWIGLU_ALPHA * gate))
            yacc[rows, :] += jnp.dot(act.astype(BF16), w2_b[...], preferred_element_type=F32)

        def multi(p, c):
            for u in range(MOE_TILES_PER_ITER):
                mm_tile(p * MOE_TILES_PER_ITER + u, MOE_TM)
            return c

        n_full = nt // (MOE_TM // MOE_UNIT)
        n_multi = n_full // MOE_TILES_PER_ITER
        lax.fori_loop(0, n_multi, multi, 0)

        def single(m, c):
            mm_tile(m, MOE_TM)
            return c

        lax.fori_loop(n_multi * MOE_TILES_PER_ITER, n_full, single, 0)

        @pl.when(nt > n_full * (MOE_TM // MOE_UNIT))
        def _():
            mm_tile(n_full, MOE_UNIT)

        @pl.when(j == n_steps - 1)
        def _():
            @pl.when(g > 0)
            def _():
                drain_scatter(nt_prev)

            def emit(m):
                slot = m & 1

                @pl.when(m >= 2)
                def _():
                    wait_tile_out(slot)

                _store_token_tiles(ystage, pl.multiple_of(slot * tm * pitch, tm * pitch), yacc[tile_rows(m), :],
                                   pitch)
                for_rows(m * tm, tm, lambda r: row_copy_out(r, slot).start())

            for_tiles(emit)


def _moe_ffn(xt, table, g_exp, g_units, g_max, w1, b1, w2, b2, layer, n_tokens):
    d = w2.shape[3]
    s = d // 128
    d_ff = w2.shape[2]
    n_steps = d_ff // MOE_TN
    tn = MOE_TN

    def jj(g, j, gt):
        return jnp.where(gt[g] > 0, j, n_steps - 1)

    grid_spec = pltpu.PrefetchScalarGridSpec(
        num_scalar_prefetch=2,
        grid=(g_max, n_steps),
        in_specs=[
            pl.BlockSpec(memory_space=pl.ANY),
            pl.BlockSpec(memory_space=pl.ANY),
            pl.BlockSpec((None, None, d, tn), lambda g, j, ge, gt: (layer, ge[g], 0, jj(g, j, gt))),
            pl.BlockSpec((None, None, d, tn), lambda g, j, ge, gt: (layer, ge[g], 0, jj(g, j, gt) + n_steps)),
            pl.BlockSpec((None, None, 1, tn), lambda g, j, ge, gt: (layer, ge[g], 0, jj(g, j, gt))),
            pl.BlockSpec((None, None, 1, tn), lambda g, j, ge, gt: (layer, ge[g], 0, jj(g, j, gt) + n_steps)),
            pl.BlockSpec((None, None, tn, d), lambda g, j, ge, gt: (layer, ge[g], jj(g, j, gt), 0)),
            pl.BlockSpec((None, None, 1, d), lambda g, j, ge, gt: (layer, ge[g], 0, 0)),
        ],
        out_specs=pl.BlockSpec(memory_space=pl.ANY),
        scratch_shapes=[
            pltpu.SMEM((2 * MOE_CAP,), I32),
            pltpu.VMEM((MOE_CAP * MOE_VMEM_PITCH, 128), F32),
            pltpu.VMEM((MOE_CAP, d), BF16),
            pltpu.VMEM((MOE_CAP, d), F32),
            pltpu.VMEM((2 * MOE_UNIT * MOE_VMEM_PITCH, 128), F32),
            pltpu.VMEM((d, tn), BF16),
            pltpu.VMEM((d, tn), BF16),
            pltpu.VMEM((tn, d), BF16),
            pltpu.SemaphoreType.DMA,
            pltpu.SemaphoreType.DMA,
            pltpu.SemaphoreType.DMA((2,)),
        ],
    )
    return pl.pallas_call(
        functools.partial(_moe_kernel, n_tokens=n_tokens, n_steps=n_steps),
        grid_spec=grid_spec,
        out_shape=jax.ShapeDtypeStruct(((TOP_K * n_tokens + MOE_UNIT) * s, 128), F32),
        compiler_params=_cparams(("arbitrary", "arbitrary")),
        name="moe_ffn",
    )(g_exp, g_units, table, xt, w1, w1, b1, b1, w2, b2)


def _combine_kernel(y0_ref, y1_ref, y2_ref, y3_ref, wt_ref, x_ref, g_ref, b_ref, o_ref, ob_ref, z_ref, *, alpha):
    tm, d = x_ref.shape
    s = d // 128
    wt = wt_ref[...]
    slots = [_load_token_tiles(y_ref, 0, tm, s) for y_ref in (y0_ref, y1_ref, y2_ref, y3_ref)]
    for a in range(s):
        cs = slice(a * 128, (a + 1) * 128)
        y = (wt[:, 0:1] * slots[0][a] + wt[:, 1:2] * slots[1][a]
             + wt[:, 2:3] * slots[2][a] + wt[:, 3:4] * slots[3][a])
        z_ref[:, cs] = alpha * x_ref[:, cs] + y
    z = _layer_norm_rows(z_ref[...], g_ref[...], b_ref[...])
    o_ref[...] = z
    ob_ref[...] = z.astype(BF16)


def _combine_ln(y4, wts, x, g, b, alpha, tm=256):
    t, d = x.shape
    s = d // 128
    nb = t // tm
    row = pl.BlockSpec((tm, d), lambda i: (i, 0))
    vec = pl.BlockSpec((1, d), lambda i: (0, 0))
    slot = [pl.BlockSpec((tm * s, 128), functools.partial(lambda i, k: (k * nb + i, 0), k=k))
            for k in range(TOP_K)]
    return pl.pallas_call(
        functools.partial(_combine_kernel, alpha=alpha),
        grid=(nb,),
        in_specs=slot + [pl.BlockSpec((tm, 128), lambda i: (i, 0)), row, vec, vec],
        out_specs=[row, row],
        out_shape=[jax.ShapeDtypeStruct((t, d), F32), jax.ShapeDtypeStruct((t, d), BF16)],
        scratch_shapes=[pltpu.VMEM((tm, d), F32)],
        compiler_params=_cparams(("arbitrary",)),
        name="combine_layernorm",
    )(y4, y4, y4, y4, wts, x, g, b)


def kernel(x, w_in, b_fgate, conv_w, lower_bounds, head_norm_g, w_out, ln1_g, ln1_b, w_router, b_router,
           w1, b1, w2, b2, ln2_g, ln2_b):
    batch, seq, d = x.shape
    depth = w_in.shape[0]
    t = batch * seq
    alpha = (2 * depth) ** 0.25

    lb = jnp.cumsum(jax.nn.softmax(lower_bounds.astype(F32), axis=0), axis=0)
    lb = lb - lb[0]
    b1r = b1.reshape(depth, N_EXPERTS, 1, b1.shape[-1])
    b2r = b2.reshape(depth, N_EXPERTS, 1, b2.shape[-1])

    xf = x.reshape(t, d)
    xb = xf.astype(BF16)
    for l in range(depth):
        w_f = jnp.zeros((d, 128), F32).at[:, :ATT_HEADS].set(w_in[l, :, ATT_COLS:REST_COL0])
        qkv = _matmul(xb, w_in, 512, 1024, F32, "in_proj_att", layer=l, n_cols=ATT_COLS)
        f_logit = _matmul(xb, w_f, 512, 128, F32, "in_proj_forget")
        proj = _matmul(xb, w_in[l, :, REST_COL0:], 512, 896, F32, "in_proj_rest")
        bf_pad = jnp.zeros((1, 128), F32).at[0, :ATT_HEADS].set(b_fgate[l])
        ccol = _forget_cumsum(f_logit, bf_pad, batch, seq)
        qa, ka, vt = _att_prep(qkv, ccol, batch, seq)
        o_att = _attention(qa, ka, vt, batch, seq)
        o_conv = _short_conv(proj, conv_w[l], batch, seq)
        o_rec = _hgrn2(proj, lb[l][None, :], batch, seq)
        o = _head_norm(o_att, o_conv, o_rec, proj, head_norm_g[l][None, :])
        mix = _matmul(o, w_out, 512, 1024, F32, "out_proj", layer=l)
        x1, x1t = _add_ln(xf, mix, ln1_g[l][None, :], ln1_b[l][None, :], alpha)

        wr_pad = jnp.zeros((d, 128), F32).at[:, :N_EXPERTS].set(w_router[l])
        br_pad = jnp.zeros((1, 128), F32).at[0, :N_EXPERTS].set(b_router[l])
        idx, wts = _router(x1, wr_pad, br_pad)
        table, g_exp, g_units, g_max = _moe_plan(idx, t)
        y4 = _moe_ffn(x1t, table, g_exp, g_units, g_max, w1, b1r, w2, b2r, l, t)
        xf, xb = _combine_ln(y4, wts, x1, ln2_g[l][None, :], ln2_b[l][None, :], alpha)
    return xf.reshape(batch, seq, d)
```

```python
import functools

import jax
import jax.numpy as jnp
import numpy as np
from jax import lax
from jax.experimental import pallas as pl
from jax.experimental.pallas import tpu as pltpu

F32 = jnp.float32
BF16 = jnp.bfloat16
I32 = jnp.int32
U32 = jnp.uint32

HEAD_DIM = 128
ATT_HEADS = 8
CONV_K = 3
HGRN_HEADS = 4
HGRN_CHUNK = 16
N_EXPERTS = 32
TOP_K = 4
SWIGLU_LIMIT = 7.0
SWIGLU_ALPHA = 1.702
LN_EPS = 1e-5
RMS_EPS = 1e-6
NEG_INF = float("-inf")
LOG2_E = 1.4426950408889634

V7X_VMEM_BYTES = 64 * 1024 * 1024
VMEM_LIMIT = 56 * 1024 * 1024

ATT_COLS = 3 * ATT_HEADS * HEAD_DIM
REST_COL0 = ATT_COLS + ATT_HEADS
COL512_CONV_B, COL512_CONV_C, COL512_CONV_H = 0, 1, 2
COL512_HQ, COL512_HF, COL512_HI, COL512_HG = 3, 4, 5, 6


def _cparams(sem):
    return pltpu.CompilerParams(dimension_semantics=sem, vmem_limit_bytes=VMEM_LIMIT)


def _log_sigmoid(z):
    return jnp.minimum(z, 0.0) - jnp.log1p(jnp.exp(-jnp.abs(z)))


def _mm_kernel(x_ref, w_ref, o_ref, wb_ref):
    @pl.when(pl.program_id(1) == 0)
    def _():
        wb_ref[...] = w_ref[...].astype(BF16)

    o_ref[...] = jnp.dot(x_ref[...].astype(BF16), wb_ref[...],
                         preferred_element_type=F32).astype(o_ref.dtype)


def _matmul(x, w, tm, tn, out_dtype, name, layer=None, n_cols=None):
    m, k = x.shape
    n = n_cols or w.shape[-1]
    if layer is None:
        w_spec = pl.BlockSpec((k, tn), lambda j, i: (0, j))
    else:
        w_spec = pl.BlockSpec((None, k, tn), lambda j, i: (layer, 0, j))
    return pl.pallas_call(
        _mm_kernel,
        grid=(n // tn, m // tm),
        in_specs=[pl.BlockSpec((tm, k), lambda j, i: (i, 0)), w_spec],
        out_specs=pl.BlockSpec((tm, tn), lambda j, i: (i, j)),
        out_shape=jax.ShapeDtypeStruct((m, n), out_dtype),
        scratch_shapes=[pltpu.VMEM((k, tn), BF16)],
        compiler_params=_cparams(("arbitrary", "arbitrary")),
        name=name,
    )(x, w)


def _fcum_kernel(f_ref, b_ref, ccol_ref, *, seq, blk):
    ls = _log_sigmoid(f_ref[...] + b_ref[...])
    r = lax.broadcasted_iota(I32, (blk, blk), 0)
    c = lax.broadcasted_iota(I32, (blk, blk), 1)
    tri = (r >= c).astype(F32)
    carry = jnp.zeros((1, 128), F32)
    for i in range(seq // blk):
        cs = jnp.dot(tri, ls[i * blk:(i + 1) * blk], precision=lax.Precision.HIGHEST,
                     preferred_element_type=F32) + carry
        ccol_ref[i * blk:(i + 1) * blk, :] = cs
        carry = cs[blk - 1:blk, :]


def _forget_cumsum(proj, b_fgate_pad, batch, seq):
    t = batch * seq
    return pl.pallas_call(
        functools.partial(_fcum_kernel, seq=seq, blk=256),
        grid=(batch,),
        in_specs=[pl.BlockSpec((seq, 128), lambda b: (b, 0)),
                  pl.BlockSpec((1, 128), lambda b: (0, 0))],
        out_specs=pl.BlockSpec((seq, 128), lambda b: (b, 0)),
        out_shape=jax.ShapeDtypeStruct((t, 128), F32),
        compiler_params=_cparams(("arbitrary",)),
        name="forget_cumsum",
    )(proj, b_fgate_pad)


ATT_AUG = 2 * HEAD_DIM


def _split3_bf16(c):
    hi = c.astype(BF16).astype(F32)
    r1 = c - hi
    mid = r1.astype(BF16).astype(F32)
    lo = (r1 - mid).astype(BF16).astype(F32)
    return hi, mid, lo


def _attprep_kernel(q_ref, k_ref, v_ref, cc_ref, qa_ref, ka_ref, vb_ref):
    scale = HEAD_DIM ** -0.5 * LOG2_E
    tm = q_ref.shape[0]
    lane = lax.broadcasted_iota(I32, (tm, HEAD_DIM), 1)
    vb_ref[...] = v_ref[...].T.astype(BF16)
    for h in range(ATT_HEADS):
        hs = slice(h * HEAD_DIM, (h + 1) * HEAD_DIM)
        c = jnp.broadcast_to(cc_ref[:, h:h + 1] * LOG2_E, (tm, HEAD_DIM))
        hi, mid, lo = _split3_bf16(c)
        one = jnp.ones_like(c)
        zero = jnp.zeros_like(c)
        q_aug = jnp.where(lane == 0, hi, jnp.where(lane == 1, mid, jnp.where(lane == 2, lo,
                          jnp.where(lane < 6, one, zero))))
        k_aug = jnp.where(lane < 3, one, jnp.where(lane == 3, -hi, jnp.where(lane == 4, -mid,
                          jnp.where(lane == 5, -lo, zero))))
        qa_ref[:, h * ATT_AUG:h * ATT_AUG + HEAD_DIM] = (q_ref[:, hs] * scale).astype(BF16)
        qa_ref[:, h * ATT_AUG + HEAD_DIM:(h + 1) * ATT_AUG] = q_aug.astype(BF16)
        ka_ref[:, h * ATT_AUG:h * ATT_AUG + HEAD_DIM] = k_ref[:, hs].astype(BF16)
        ka_ref[:, h * ATT_AUG + HEAD_DIM:(h + 1) * ATT_AUG] = k_aug.astype(BF16)


def _att_prep(proj, ccol, batch, seq, tm=512):
    t = proj.shape[0]
    ns = seq // tm
    width = ATT_HEADS * HEAD_DIM
    return pl.pallas_call(
        _attprep_kernel,
        grid=(batch, ns),
        in_specs=[pl.BlockSpec((tm, width), lambda b, i: (b * ns + i, 0)),
                  pl.BlockSpec((tm, width), lambda b, i: (b * ns + i, 1)),
                  pl.BlockSpec((tm, width), lambda b, i: (b * ns + i, 2)),
                  pl.BlockSpec((tm, 128), lambda b, i: (b * ns + i, 0))],
        out_specs=[pl.BlockSpec((tm, ATT_HEADS * ATT_AUG), lambda b, i: (b * ns + i, 0)),
                   pl.BlockSpec((tm, ATT_HEADS * ATT_AUG), lambda b, i: (b * ns + i, 0)),
                   pl.BlockSpec((None, width, tm), lambda b, i: (b, 0, i))],
        out_shape=[jax.ShapeDtypeStruct((t, ATT_HEADS * ATT_AUG), BF16),
                   jax.ShapeDtypeStruct((t, ATT_HEADS * ATT_AUG), BF16),
                   jax.ShapeDtypeStruct((batch, width, seq), BF16)],
        compiler_params=_cparams(("arbitrary", "arbitrary")),
        name="att_prep",
    )(proj, proj, proj, ccol)


def _att_kernel(qa_ref, ka_ref, vt_ref, o_ref, m_ref, l_ref, acc_ref, *, tq, tk, nk):
    i = pl.program_id(1)
    j = pl.program_id(2)

    @pl.when(j == 0)
    def _():
        m_ref[...] = jnp.full(m_ref.shape, NEG_INF, F32)
        l_ref[...] = jnp.zeros(l_ref.shape, F32)
        acc_ref[...] = jnp.zeros(acc_ref.shape, F32)

    def step(masked):
        if masked:
            mask = (lax.broadcasted_iota(I32, (tk, tq), 0) <= lax.broadcasted_iota(I32, (tk, tq), 1))
        def scores(h):
            as_ = slice(h * ATT_AUG, (h + 1) * ATT_AUG)
            s = lax.dot_general(ka_ref[:, as_], qa_ref[:, as_], (((1,), (1,)), ((), ())),
                                preferred_element_type=F32)
            return jnp.where(mask, s, NEG_INF) if masked else s

        s_next = scores(0)
        for h in range(ATT_HEADS):
            hs = slice(h * HEAD_DIM, (h + 1) * HEAD_DIM)
            s = s_next
            if h + 1 < ATT_HEADS:
                s_next = scores(h + 1)
            m_prev = m_ref[h:h + 1, :]
            m_new = jnp.maximum(m_prev, jnp.max(s, axis=0, keepdims=True))
            p = jnp.exp2(s - m_new)
            alpha = jnp.exp2(m_prev - m_new)
            l_ref[h:h + 1, :] = alpha * l_ref[h:h + 1, :] + jnp.sum(p, axis=0, keepdims=True)
            acc_ref[hs, :] = alpha * acc_ref[hs, :] + jnp.dot(vt_ref[hs, :], p.astype(BF16),
                                                            preferred_element_type=F32)
            m_ref[h:h + 1, :] = m_new

    @pl.when(j < i)
    def _():
        step(False)

    @pl.when(j == i)
    def _():
        step(True)

    @pl.when(j == nk - 1)
    def _():
        for h in range(ATT_HEADS):
            hs = slice(h * HEAD_DIM, (h + 1) * HEAD_DIM)
            o_ref[:, hs] = (acc_ref[hs, :] / l_ref[h:h + 1, :]).T


def _attention(qa, ka, vt, batch, seq, tq=512):
    t = batch * seq
    nq = seq // tq
    width = ATT_HEADS * HEAD_DIM
    kernel = functools.partial(_att_kernel, tq=tq, tk=tq, nk=nq)
    return pl.pallas_call(
        kernel,
        grid=(batch, nq, nq),
        in_specs=[pl.BlockSpec((tq, ATT_HEADS * ATT_AUG), lambda b, i, j: (b * nq + i, 0)),
                  pl.BlockSpec((tq, ATT_HEADS * ATT_AUG), lambda b, i, j: (b * nq + jnp.minimum(j, i), 0)),
                  pl.BlockSpec((None, width, tq), lambda b, i, j: (b, 0, jnp.minimum(j, i)))],
        out_specs=pl.BlockSpec((tq, width), lambda b, i, j: (b * nq + i, 0)),
        out_shape=jax.ShapeDtypeStruct((t, width), F32),
        scratch_shapes=[pltpu.VMEM((ATT_HEADS, tq), F32),
                        pltpu.VMEM((ATT_HEADS, tq), F32),
                        pltpu.VMEM((width, tq), F32)],
        compiler_params=_cparams(("arbitrary", "arbitrary", "arbitrary")),
        name="fox_attention",
    )(qa, ka, vt)


def _conv_kernel(b_ref, c_ref, h_ref, w_ref, o_ref, carry_ref, *, ts):
    @pl.when(pl.program_id(1) == 0)
    def _():
        carry_ref[...] = jnp.zeros(carry_ref.shape, F32)

    u = c_ref[...] * h_ref[...]
    ext = jnp.concatenate([carry_ref[...], u], axis=0)
    u1 = ext[7:7 + ts]
    u2 = ext[6:6 + ts]
    w = w_ref[...]
    y = w[2:3] * u + w[1:2] * u1 + w[0:1] * u2
    o_ref[...] = b_ref[...] * y
    carry_ref[...] = u[ts - 8:ts]


def _short_conv(proj, conv_w, batch, seq, ts=512):
    t = batch * seq
    ns = seq // ts
    w = conv_w.shape[1]
    return pl.pallas_call(
        functools.partial(_conv_kernel, ts=ts),
        grid=(batch, ns),
        in_specs=[pl.BlockSpec((ts, w), lambda b, i: (b * ns + i, COL512_CONV_B)),
                  pl.BlockSpec((ts, w), lambda b, i: (b * ns + i, COL512_CONV_C)),
                  pl.BlockSpec((ts, w), lambda b, i: (b * ns + i, COL512_CONV_H)),
                  pl.BlockSpec((CONV_K, w), lambda b, i: (0, 0))],
        out_specs=pl.BlockSpec((ts, w), lambda b, i: (b * ns + i, 0)),
        out_shape=jax.ShapeDtypeStruct((t, w), F32),
        scratch_shapes=[pltpu.VMEM((8, w), F32)],
        compiler_params=_cparams(("arbitrary", "arbitrary")),
        name="short_conv",
    )(proj, proj, proj, conv_w)


def _hgrn_kernel(q_ref, f_ref, i_ref, lb_ref, o_ref, st_ref, qs_ref, kk_ref, bc_ref, *, bs):
    c = HGRN_CHUNK

    @pl.when(pl.program_id(1) == 0)
    def _():
        st_ref[...] = jnp.zeros(st_ref.shape, F32)

    z = f_ref[...]
    lb = lb_ref[...]
    ls = _log_sigmoid(z)
    a = jnp.log(lb)
    b = jnp.log1p(-lb) + ls
    log_f = jnp.maximum(a, b) + jnp.log1p(jnp.exp(-jnp.abs(a - b)))
    kk_ref[...] = (1.0 - lb) * jnp.exp(ls - z)
    q = q_ref[...]
    qs_ref[...] = q / (1.0 + jnp.exp(-q))
    r = lax.broadcasted_iota(I32, (bs, bs), 0)
    cc = lax.broadcasted_iota(I32, (bs, bs), 1)
    tri = ((r // c == cc // c) & (r >= cc)).astype(F32)
    bc_ref[...] = jnp.dot(tri, log_f, precision=lax.Precision.HIGHEST, preferred_element_type=F32)

    t_idx = lax.broadcasted_iota(I32, (c, HEAD_DIM), 0)
    ones = jnp.ones((HEAD_DIM, HEAD_DIM), BF16)

    def chunk(n, carry):
        rows = pl.ds(pl.multiple_of(n * c, c), c)
        for h in range(HGRN_HEADS):
            hs = slice(h * HEAD_DIM, (h + 1) * HEAD_DIM)
            qh = qs_ref[rows, hs]
            kh = kk_ref[rows, hs]
            vh = i_ref[rows, hs]
            bh = bc_ref[rows, hs]
            b_last = bh[c - 1:c]
            parts = []
            for s in range(c):
                d = jnp.where(t_idx >= s, bh - bh[s:s + 1], NEG_INF)
                parts.append(qh * jnp.exp(d) * kh[s:s + 1])
            a3 = jnp.concatenate(parts, axis=0).astype(BF16)
            rs = jnp.dot(a3, ones, preferred_element_type=F32)
            o = lax.dot_general((qh * jnp.exp(bh)).astype(BF16), st_ref[h].astype(BF16),
                                (((1,), (1,)), ((), ())), preferred_element_type=F32)
            for s in range(c):
                o = o + rs[s * c:(s + 1) * c] * vh[s:s + 1]
            o_ref[rows, hs] = o
            kd = kh * jnp.exp(b_last - bh)
            kv_t = lax.dot_general(vh.astype(BF16), kd.astype(BF16), (((0,), (0,)), ((), ())),
                                   preferred_element_type=F32)
            st_ref[h] = jnp.exp(b_last) * st_ref[h] + kv_t
        return carry

    lax.fori_loop(0, bs // c, chunk, 0, unroll=4)


def _hgrn2(proj, lb, batch, seq, bs=256):
    t = batch * seq
    ns = seq // bs
    w = HGRN_HEADS * HEAD_DIM
    return pl.pallas_call(
        functools.partial(_hgrn_kernel, bs=bs),
        grid=(batch, ns),
        in_specs=[pl.BlockSpec((bs, w), lambda b, i: (b * ns + i, COL512_HQ)),
                  pl.BlockSpec((bs, w), lambda b, i: (b * ns + i, COL512_HF)),
                  pl.BlockSpec((bs, w), lambda b, i: (b * ns + i, COL512_HI)),
                  pl.BlockSpec((1, w), lambda b, i: (0, 0))],
        out_specs=pl.BlockSpec((bs, w), lambda b, i: (b * ns + i, 0)),
        out_shape=jax.ShapeDtypeStruct((t, w), F32),
        scratch_shapes=[pltpu.VMEM((HGRN_HEADS, HEAD_DIM, HEAD_DIM), F32),
                        pltpu.VMEM((bs, w), F32),
                        pltpu.VMEM((bs, w), F32),
                        pltpu.VMEM((bs, w), F32)],
        compiler_params=_cparams(("arbitrary", "arbitrary")),
        name="hgrn2",
    )(proj, proj, proj, lb)


def _headnorm_kernel(a_ref, c_ref, r_ref, hg_ref, g_ref, o_ref):
    na = a_ref.shape[1] // HEAD_DIM
    nc = c_ref.shape[1] // HEAD_DIM
    nr = r_ref.shape[1] // HEAD_DIM

    def rms(x, g):
        ms = jnp.mean(x * x, axis=-1, keepdims=True)
        return x * lax.rsqrt(ms + RMS_EPS) * g

    col = 0
    for src, n, gated in ((a_ref, na, False), (c_ref, nc, False), (r_ref, nr, True)):
        for h in range(n):
            hs = slice(h * HEAD_DIM, (h + 1) * HEAD_DIM)
            os_ = slice(col, col + HEAD_DIM)
            y = rms(src[:, hs], g_ref[:, os_])
            if gated:
                hg = hg_ref[:, hs]
                y = y * (hg / (1.0 + jnp.exp(-hg)))
            o_ref[:, os_] = y.astype(o_ref.dtype)
            col += HEAD_DIM


def _head_norm(o_att, o_conv, o_rec, proj, g, tm=256):
    t = o_att.shape[0]
    wa, wc, wr = o_att.shape[1], o_conv.shape[1], o_rec.shape[1]
    d = wa + wc + wr
    return pl.pallas_call(
        _headnorm_kernel,
        grid=(t // tm,),
        in_specs=[pl.BlockSpec((tm, wa), lambda i: (i, 0)),
                  pl.BlockSpec((tm, wc), lambda i: (i, 0)),
                  pl.BlockSpec((tm, wr), lambda i: (i, 0)),
                  pl.BlockSpec((tm, wr), lambda i: (i, COL512_HG)),
                  pl.BlockSpec((1, d), lambda i: (0, 0))],
        out_specs=pl.BlockSpec((tm, d), lambda i: (i, 0)),
        out_shape=jax.ShapeDtypeStruct((t, d), BF16),
        compiler_params=_cparams(("arbitrary",)),
        name="head_norm",
    )(o_att, o_conv, o_rec, proj, g)


def _store_token_tiles(ref, base, y, pitch=None):
    n, d = y.shape
    s = d // 128
    for a in range(s):
        ref[pl.ds(base + a, n, stride=pitch or s), :] = y[:, a * 128:(a + 1) * 128]


def _load_token_tiles(ref, base, n, s, pitch=None):
    return [ref[pl.ds(base + a, n, stride=pitch or s), :] for a in range(s)]


def _layer_norm_rows(z, g, b):
    mu = jnp.mean(z, axis=-1, keepdims=True)
    zc = z - mu
    var = jnp.mean(zc * zc, axis=-1, keepdims=True)
    return zc * lax.rsqrt(var + LN_EPS) * g + b


def _addln_kernel(x_ref, y_ref, g_ref, b_ref, o_ref, ot_ref, *, alpha):
    y = _layer_norm_rows(alpha * x_ref[...] + y_ref[...], g_ref[...], b_ref[...])
    o_ref[...] = y
    _store_token_tiles(ot_ref, 0, y)


def _add_ln(x, y, g, b, alpha, tm=256):
    t, d = x.shape
    s = d // 128
    row = pl.BlockSpec((tm, d), lambda i: (i, 0))
    vec = pl.BlockSpec((1, d), lambda i: (0, 0))
    return pl.pallas_call(
        functools.partial(_addln_kernel, alpha=alpha),
        grid=(t // tm,),
        in_specs=[row, row, vec, vec],
        out_specs=[row, pl.BlockSpec((tm * s, 128), lambda i: (i, 0))],
        out_shape=[jax.ShapeDtypeStruct((t, d), F32),
                   jax.ShapeDtypeStruct((t * s, 128), F32)],
        compiler_params=_cparams(("arbitrary",)),
        name="add_layernorm",
    )(x, y, g, b)


def _router_kernel(x_ref, w_ref, b_ref, idx_ref, wt_ref):
    logits = jnp.dot(x_ref[...], w_ref[...], precision=lax.Precision.HIGHEST,
                     preferred_element_type=F32) + b_ref[...]
    lane = lax.broadcasted_iota(I32, logits.shape, 1)
    logits = jnp.where(lane < N_EXPERTS, logits, NEG_INF)
    vals, idxs = [], []
    for _ in range(TOP_K):
        m = jnp.max(logits, axis=-1, keepdims=True)
        sel = jnp.min(jnp.where(logits == m, lane, 128), axis=-1, keepdims=True)
        vals.append(m)
        idxs.append(sel)
        logits = jnp.where(lane == sel, NEG_INF, logits)
    es = [jnp.exp(v - vals[0]) for v in vals]
    tot = es[0] + es[1] + es[2] + es[3]
    idx_out = jnp.zeros(logits.shape, I32)
    wt_out = jnp.zeros(logits.shape, F32)
    for k in range(TOP_K):
        idx_out = jnp.where(lane == k, idxs[k], idx_out)
        wt_out = jnp.where(lane == k, es[k] / tot, wt_out)
    idx_ref[...] = idx_out
    wt_ref[...] = wt_out


def _router(x, w_pad, b_pad, tm=512):
    t, d = x.shape
    return pl.pallas_call(
        _router_kernel,
        grid=(t // tm,),
        in_specs=[pl.BlockSpec((tm, d), lambda i: (i, 0)),
                  pl.BlockSpec((d, 128), lambda i: (0, 0)),
                  pl.BlockSpec((1, 128), lambda i: (0, 0))],
        out_specs=[pl.BlockSpec((tm, 128), lambda i: (i, 0)),
                   pl.BlockSpec((tm, 128), lambda i: (i, 0))],
        out_shape=[jax.ShapeDtypeStruct((t, 128), I32),
                   jax.ShapeDtypeStruct((t, 128), F32)],
        compiler_params=_cparams(("arbitrary",)),
        name="router_top4",
    )(x, w_pad, b_pad)


MOE_UNIT = 128
MOE_TM = 256
MOE_TILES_PER_ITER = 4
MOE_TN = 256
MOE_CAP = 1280
MOE_ROW_UNROLL = 8
MOE_VMEM_PITCH = 20
MOE_VMEM_PITCH_ALIGN = 4


MOE_PLAN_GROUPS = 128
MOE_CAP_SHIFT = 26
MOE_CAP_RECIP = -(-(1 << MOE_CAP_SHIFT) // MOE_CAP)


def _plan_kernel(idx_ref, gb_ref, tbl_ref, run_ref, hi_ref, lo_ref, cnt_ref, *, n_tokens):
    i = pl.program_id(0)
    tm = idx_ref.shape[0]

    @pl.when(i == 0)
    def _():
        run_ref[...] = jnp.zeros(run_ref.shape, F32)
        hi_ref[...] = jnp.zeros(hi_ref.shape, F32)
        lo_ref[...] = jnp.zeros(lo_ref.shape, F32)
        cnt_ref[...] = jnp.zeros(cnt_ref.shape, F32)

    idx = idx_ref[...]
    lane = lax.broadcasted_iota(I32, (tm, 128), 1)
    tri = (lax.broadcasted_iota(I32, (tm, tm), 0) >= lax.broadcasted_iota(I32, (tm, tm), 1)).astype(BF16)
    g_base = gb_ref[...].astype(F32)
    run = run_ref[...]
    tok = i * tm + lax.broadcasted_iota(I32, (tm, 1), 0)
    col_g = lax.broadcasted_iota(I32, (tm, MOE_PLAN_GROUPS), 1)
    col_r = lax.broadcasted_iota(I32, (tm, MOE_CAP), 1)
    tn = (((0,), (0,)), ((), ()))
    for k in range(TOP_K):
        onehot = (lane == idx[:, k:k + 1]).astype(F32)
        prefix = jnp.dot(tri, onehot.astype(BF16), preferred_element_type=F32)
        rank = jnp.sum(onehot * (run + prefix - 1.0), axis=-1, keepdims=True).astype(I32)
        first = jnp.sum(onehot * g_base, axis=-1, keepdims=True).astype(I32)
        run = run + prefix[tm - 1:tm, :]
        sub = (rank * MOE_CAP_RECIP) >> MOE_CAP_SHIFT
        grp = first + sub
        row = rank - sub * MOE_CAP
        dst = k * n_tokens + tok
        sel_g = (col_g == grp).astype(BF16)
        sel_r = col_r == row
        d_hi = (dst >> 8).astype(F32)
        d_lo = (dst & 255).astype(F32)
        cnt_ref[...] += lax.dot_general(sel_g, sel_r.astype(BF16), tn, preferred_element_type=F32)
        hi_ref[...] += lax.dot_general(sel_g, jnp.where(sel_r, d_hi, 0.0).astype(BF16), tn,
                                       preferred_element_type=F32)
        lo_ref[...] += lax.dot_general(sel_g, jnp.where(sel_r, d_lo, 0.0).astype(BF16), tn,
                                       preferred_element_type=F32)
    run_ref[...] = run

    @pl.when(i == pl.num_programs(0) - 1)
    def _():
        trash = TOP_K * n_tokens + (lax.broadcasted_iota(I32, tbl_ref.shape, 1) & (MOE_UNIT - 1))
        val = hi_ref[...].astype(I32) * 256 + lo_ref[...].astype(I32)
        tbl_ref[...] = jnp.where(cnt_ref[...] > 0.5, val, trash)


def _moe_plan(idx, n_tokens, tm=512):
    n_pairs = n_tokens * TOP_K
    g_max = N_EXPERTS + -(-n_pairs // MOE_CAP)
    assert g_max <= MOE_PLAN_GROUPS and n_pairs <= (1 << 15) and TOP_K * n_tokens + MOE_UNIT <= (1 << 16)
    assert all((r * MOE_CAP_RECIP) >> MOE_CAP_SHIFT == r // MOE_CAP for r in range(1 << 15))
    counts = jnp.sum((idx[:, :TOP_K, None] == jnp.arange(N_EXPERTS, dtype=I32)).astype(I32), axis=(0, 1))
    n_grp = (counts + MOE_CAP - 1) // MOE_CAP
    g_end = jnp.cumsum(n_grp)
    g_base = g_end - n_grp
    gb_pad = jnp.zeros((1, 128), I32).at[0, :N_EXPERTS].set(g_base)
    acc = pltpu.VMEM((MOE_PLAN_GROUPS, MOE_CAP), F32)
    table = pl.pallas_call(
        functools.partial(_plan_kernel, n_tokens=n_tokens),
        grid=(n_tokens // tm,),
        in_specs=[pl.BlockSpec((tm, 128), lambda i: (i, 0)),
                  pl.BlockSpec((1, 128), lambda i: (0, 0))],
        out_specs=pl.BlockSpec((MOE_PLAN_GROUPS, MOE_CAP), lambda i: (0, 0)),
        out_shape=jax.ShapeDtypeStruct((MOE_PLAN_GROUPS, MOE_CAP), I32),
        scratch_shapes=[pltpu.VMEM((1, 128), F32), acc, acc, acc],
        compiler_params=_cparams(("arbitrary",)),
        name="moe_plan",
    )(idx, gb_pad)
    gid = jnp.arange(g_max, dtype=I32)
    n_used = g_end[-1]
    g_exp = jnp.sum((jnp.minimum(gid, n_used - 1)[:, None] >= g_end[None, :]).astype(I32), axis=1)
    g_rows = jnp.clip(counts[g_exp] - (gid - g_base[g_exp]) * MOE_CAP, 0, MOE_CAP)
    g_rows = jnp.where(gid < n_used, g_rows, 0)
    g_units = ((g_rows + MOE_UNIT - 1) // MOE_UNIT).astype(I32)
    return table, g_exp, g_units, g_max


def _moe_kernel(ge_ref, gt_ref, tbl_hbm, x_hbm, w1g_ref, w1u_ref, b1g_ref, b1u_ref, w2_ref, b2_ref,
                out_hbm, tbl_smem, xbuf, xb, yacc, ystage, w1g_b, w1u_b, w2_b, sem_tbl, sem_in, sem_out,
                *, n_tokens, n_steps):
    g = pl.program_id(0)
    j = pl.program_id(1)
    n_groups = pl.num_programs(0)
    nt = gt_ref[g]
    nt_prev = gt_ref[jnp.maximum(g - 1, 0)]
    g_next = jnp.minimum(g + 1, n_groups - 1)
    tm = MOE_UNIT
    d = yacc.shape[1]
    s = d // 128
    tbase = (g & 1) * MOE_CAP

    pitch = MOE_VMEM_PITCH

    def token_rows(r, n=1):
        return pl.ds(pl.multiple_of(r * s, s), n * s)

    def vmem_token(r):
        return pl.ds(pl.multiple_of(r * pitch, MOE_VMEM_PITCH_ALIGN), s)

    def start_gather(grp, n_tiles):
        base = (grp & 1) * MOE_CAP
        cp = pltpu.make_async_copy(tbl_hbm.at[grp], tbl_smem.at[pl.ds(base, MOE_CAP)], sem_tbl)
        cp.start()
        cp.wait()

        def row_copy_in(r):
            tok = tbl_smem[base + r] & (n_tokens - 1)
            pltpu.make_async_copy(x_hbm.at[token_rows(tok)], xbuf.at[vmem_token(r)], sem_in).start()

        for_rows(0, n_tiles * tm, row_copy_in)

    def row_copy_out(r, slot):
        src = ystage.at[vmem_token(slot * tm + (r & (tm - 1)))]
        return pltpu.make_async_copy(src, out_hbm.at[token_rows(tbl_smem[tbase + r])], sem_out.at[slot])

    def wait_tile_out(slot):
        pltpu.make_async_copy(ystage.at[token_rows(0, tm)], out_hbm.at[token_rows(0, tm)],
                              sem_out.at[slot]).wait()

    def drain_scatter(n_tiles):
        @pl.when(n_tiles >= 2)
        def _():
            wait_tile_out(n_tiles & 1)

        wait_tile_out((n_tiles - 1) & 1)

    def for_rows(first, n_rows, fn):
        def body(i, c):
            for u in range(MOE_ROW_UNROLL):
                fn(first + i * MOE_ROW_UNROLL + u)
            return c

        lax.fori_loop(0, n_rows // MOE_ROW_UNROLL, body, 0)

    def for_tiles(fn):
        def body(m, c):
            fn(m)
            return c

        lax.fori_loop(0, nt, body, 0)

    def tile_rows(m):
        return pl.ds(pl.multiple_of(m * tm, tm), tm)

    @pl.when((g == 0) & (j == 0))
    def _():
        ystage[pl.ds(0, tm * s), :] = jnp.zeros((tm * s, 128), F32)
        cp = pltpu.make_async_copy(ystage.at[token_rows(0, tm)],
                                   out_hbm.at[token_rows(TOP_K * n_tokens, tm)], sem_out.at[0])
        cp.start()
        cp.wait()
        start_gather(g, nt)

    @pl.when((nt == 0) & (j == 0) & (g > 0) & (nt_prev > 0))
    def _():
        drain_scatter(nt_prev)

    @pl.when(nt > 0)
    def _():
        @pl.when(j == 0)
        def _():
            def init_acc(m):
                yacc[tile_rows(m), :] = jnp.broadcast_to(b2_ref[...], (tm, d))

            for_tiles(init_acc)

            def wait_rows(m):
                pltpu.make_async_copy(x_hbm.at[token_rows(0, tm)], xbuf.at[token_rows(0, tm)], sem_in).wait()

            for_tiles(wait_rows)

            def to_rows(m):
                parts = _load_token_tiles(xbuf, pl.multiple_of(m * tm * pitch, tm * pitch), tm, s, pitch)
                for a in range(s):
                    xb[tile_rows(m), a * 128:(a + 1) * 128] = parts[a].astype(BF16)

            for_tiles(to_rows)

            @pl.when((g + 1 < n_groups) & (gt_ref[g_next] > 0))
            def _():
                start_gather(g_next, gt_ref[g_next])

        w1g_b[...] = w1g_ref[...].astype(BF16)
        w1u_b[...] = w1u_ref[...].astype(BF16)
        w2_b[...] = w2_ref[...].astype(BF16)

        def mm_tile(m, n_rows):
            rows = pl.ds(pl.multiple_of(m * MOE_TM, MOE_TM), n_rows)
            x = xb[rows, :]
            hg = jnp.dot(x, w1g_b[...], preferred_element_type=F32) + b1g_ref[...]
            hu = jnp.dot(x, w1u_b[...], preferred_element_type=F32) + b1u_ref[...]
            gate = jnp.minimum(hg, SWIGLU_LIMIT)
            up = jnp.clip(hu, -SWIGLU_LIMIT, SWIGLU_LIMIT)
            act = (up + 1.0) * gate / (1.0 + jnp.exp(-SWIGLU_ALPHA * gate))
            yacc[rows, :] += jnp.dot(act.astype(BF16), w2_b[...], preferred_element_type=F32)

        def multi(p, c):
            for u in range(MOE_TILES_PER_ITER):
                mm_tile(p * MOE_TILES_PER_ITER + u, MOE_TM)
            return c

        n_full = nt // (MOE_TM // MOE_UNIT)
        n_multi = n_full // MOE_TILES_PER_ITER
        lax.fori_loop(0, n_multi, multi, 0)

        def single(m, c):
            mm_tile(m, MOE_TM)
            return c

        lax.fori_loop(n_multi * MOE_TILES_PER_ITER, n_full, single, 0)

        @pl.when(nt > n_full * (MOE_TM // MOE_UNIT))
        def _():
            mm_tile(n_full, MOE_UNIT)

        @pl.when(j == n_steps - 1)
        def _():
            @pl.when(g > 0)
            def _():
                drain_scatter(nt_prev)

            def emit(m):
                slot = m & 1

                @pl.when(m >= 2)
                def _():
                    wait_tile_out(slot)

                _store_token_tiles(ystage, pl.multiple_of(slot * tm * pitch, tm * pitch), yacc[tile_rows(m), :],
                                   pitch)
                for_rows(m * tm, tm, lambda r: row_copy_out(r, slot).start())

            for_tiles(emit)


def _moe_ffn(xt, table, g_exp, g_units, g_max, w1, b1, w2, b2, layer, n_tokens):
    d = w2.shape[3]
    s = d // 128
    d_ff = w2.shape[2]
    n_steps = d_ff // MOE_TN
    tn = MOE_TN

    def jj(g, j, gt):
        return jnp.where(gt[g] > 0, j, n_steps - 1)

    grid_spec = pltpu.PrefetchScalarGridSpec(
        num_scalar_prefetch=2,
        grid=(g_max, n_steps),
        in_specs=[
            pl.BlockSpec(memory_space=pl.ANY),
            pl.BlockSpec(memory_space=pl.ANY),
            pl.BlockSpec((None, None, d, tn), lambda g, j, ge, gt: (layer, ge[g], 0, jj(g, j, gt))),
            pl.BlockSpec((None, None, d, tn), lambda g, j, ge, gt: (layer, ge[g], 0, jj(g, j, gt) + n_steps)),
            pl.BlockSpec((None, None, 1, tn), lambda g, j, ge, gt: (layer, ge[g], 0, jj(g, j, gt))),
            pl.BlockSpec((None, None, 1, tn), lambda g, j, ge, gt: (layer, ge[g], 0, jj(g, j, gt) + n_steps)),
            pl.BlockSpec((None, None, tn, d), lambda g, j, ge, gt: (layer, ge[g], jj(g, j, gt), 0)),
            pl.BlockSpec((None, None, 1, d), lambda g, j, ge, gt: (layer, ge[g], 0, 0)),
        ],
        out_specs=pl.BlockSpec(memory_space=pl.ANY),
        scratch_shapes=[
            pltpu.SMEM((2 * MOE_CAP,), I32),
            pltpu.VMEM((MOE_CAP * MOE_VMEM_PITCH, 128), F32),
            pltpu.VMEM((MOE_CAP, d), BF16),
            pltpu.VMEM((MOE_CAP, d), F32),
            pltpu.VMEM((2 * MOE_UNIT * MOE_VMEM_PITCH, 128), F32),
            pltpu.VMEM((d, tn), BF16),
            pltpu.VMEM((d, tn), BF16),
            pltpu.VMEM((tn, d), BF16),
            pltpu.SemaphoreType.DMA,
            pltpu.SemaphoreType.DMA,
            pltpu.SemaphoreType.DMA((2,)),
        ],
    )
    return pl.pallas_call(
        functools.partial(_moe_kernel, n_tokens=n_tokens, n_steps=n_steps),
        grid_spec=grid_spec,
        out_shape=jax.ShapeDtypeStruct(((TOP_K * n_tokens + MOE_UNIT) * s, 128), F32),
        compiler_params=_cparams(("arbitrary", "arbitrary")),
        name="moe_ffn",
    )(g_exp, g_units, table, xt, w1, w1, b1, b1, w2, b2)


def _combine_kernel(y0_ref, y1_ref, y2_ref, y3_ref, wt_ref, x_ref, g_ref, b_ref, o_ref, ob_ref, z_ref, *, alpha):
    tm, d = x_ref.shape
    s = d // 128
    wt = wt_ref[...]
    slots = [_load_token_tiles(y_ref, 0, tm, s) for y_ref in (y0_ref, y1_ref, y2_ref, y3_ref)]
    for a in range(s):
        cs = slice(a * 128, (a + 1) * 128)
        y = (wt[:, 0:1] * slots[0][a] + wt[:, 1:2] * slots[1][a]
             + wt[:, 2:3] * slots[2][a] + wt[:, 3:4] * slots[3][a])
        z_ref[:, cs] = alpha * x_ref[:, cs] + y
    z = _layer_norm_rows(z_ref[...], g_ref[...], b_ref[...])
    o_ref[...] = z
    ob_ref[...] = z.astype(BF16)


def _combine_ln(y4, wts, x, g, b, alpha, tm=256):
    t, d = x.shape
    s = d // 128
    nb = t // tm
    row = pl.BlockSpec((tm, d), lambda i: (i, 0))
    vec = pl.BlockSpec((1, d), lambda i: (0, 0))
    slot = [pl.BlockSpec((tm * s, 128), functools.partial(lambda i, k: (k * nb + i, 0), k=k))
            for k in range(TOP_K)]
    return pl.pallas_call(
        functools.partial(_combine_kernel, alpha=alpha),
        grid=(nb,),
        in_specs=slot + [pl.BlockSpec((tm, 128), lambda i: (i, 0)), row, vec, vec],
        out_specs=[row, row],
        out_shape=[jax.ShapeDtypeStruct((t, d), F32), jax.ShapeDtypeStruct((t, d), BF16)],
        scratch_shapes=[pltpu.VMEM((tm, d), F32)],
        compiler_params=_cparams(("arbitrary",)),
        name="combine_layernorm",
    )(y4, y4, y4, y4, wts, x, g, b)


def kernel(x, w_in, b_fgate, conv_w, lower_bounds, head_norm_g, w_out, ln1_g, ln1_b, w_router, b_router,
           w1, b1, w2, b2, ln2_g, ln2_b):
    batch, seq, d = x.shape
    depth = w_in.shape[0]
    t = batch * seq
    alpha = (2 * depth) ** 0.25

    lb = jnp.cumsum(jax.nn.softmax(lower_bounds.astype(F32), axis=0), axis=0)
    lb = lb - lb[0]
    b1r = b1.reshape(depth, N_EXPERTS, 1, b1.shape[-1])
    b2r = b2.reshape(depth, N_EXPERTS, 1, b2.shape[-1])

    xf = x.reshape(t, d)
    xb = xf.astype(BF16)
    for l in range(depth):
        w_f = jnp.zeros((d, 128), F32).at[:, :ATT_HEADS].set(w_in[l, :, ATT_COLS:REST_COL0])
        qkv = _matmul(xb, w_in, 512, 1536, F32, "in_proj_att", layer=l, n_cols=ATT_COLS)
        f_logit = _matmul(xb, w_f, 512, 128, F32, "in_proj_forget")
        proj = _matmul(xb, w_in[l, :, REST_COL0:], 512, 1792, F32, "in_proj_rest")
        bf_pad = jnp.zeros((1, 128), F32).at[0, :ATT_HEADS].set(b_fgate[l])
        ccol = _forget_cumsum(f_logit, bf_pad, batch, seq)
        qa, ka, vt = _att_prep(qkv, ccol, batch, seq)
        o_att = _attention(qa, ka, vt, batch, seq)
        o_conv = _short_conv(proj, conv_w[l], batch, seq)
        o_rec = _hgrn2(proj, lb[l][None, :], batch, seq)
        o = _head_norm(o_att, o_conv, o_rec, proj, head_norm_g[l][None, :])
        mix = _matmul(o, w_out, 512, 1024, F32, "out_proj", layer=l)
        x1, x1t = _add_ln(xf, mix, ln1_g[l][None, :], ln1_b[l][None, :], alpha)

        wr_pad = jnp.zeros((d, 128), F32).at[:, :N_EXPERTS].set(w_router[l])
        br_pad = jnp.zeros((1, 128), F32).at[0, :N_EXPERTS].set(b_router[l])
        idx, wts = _router(x1, wr_pad, br_pad)
        table, g_exp, g_units, g_max = _moe_plan(idx, t)
        y4 = _moe_ffn(x1t, table, g_exp, g_units, g_max, w1, b1r, w2, b2r, l, t)
        xf, xb = _combine_ln(y4, wts, x1, ln2_g[l][None, :], ln2_b[l][None, :], alpha)
    return xf.reshape(batch, seq, d)
```

```python
import functools

import jax
import jax.numpy as jnp
import numpy as np
from jax import lax
from jax.experimental import pallas as pl
from jax.experimental.pallas import tpu as pltpu

F32 = jnp.float32
BF16 = jnp.bfloat16
I32 = jnp.int32
U32 = jnp.uint32

HEAD_DIM = 128
ATT_HEADS = 8
CONV_K = 3
HGRN_HEADS = 4
HGRN_CHUNK = 16
N_EXPERTS = 32
TOP_K = 4
SWIGLU_LIMIT = 7.0
SWIGLU_ALPHA = 1.702
LN_EPS = 1e-5
RMS_EPS = 1e-6
NEG_INF = float("-inf")
LOG2_E = 1.4426950408889634

V7X_VMEM_BYTES = 64 * 1024 * 1024
VMEM_LIMIT = 56 * 1024 * 1024

ATT_COLS = 3 * ATT_HEADS * HEAD_DIM
REST_COL0 = ATT_COLS + ATT_HEADS
COL512_CONV_B, COL512_CONV_C, COL512_CONV_H = 0, 1, 2
COL512_HQ, COL512_HF, COL512_HI, COL512_HG = 3, 4, 5, 6


def _cparams(sem):
    return pltpu.CompilerParams(dimension_semantics=sem, vmem_limit_bytes=VMEM_LIMIT)


def _log_sigmoid(z):
    return jnp.minimum(z, 0.0) - jnp.log1p(jnp.exp(-jnp.abs(z)))


def _mm_kernel(x_ref, w_ref, o_ref, wb_ref):
    @pl.when(pl.program_id(1) == 0)
    def _():
        wb_ref[...] = w_ref[...].astype(BF16)

    o_ref[...] = jnp.dot(x_ref[...].astype(BF16), wb_ref[...],
                         preferred_element_type=F32).astype(o_ref.dtype)


def _matmul(x, w, tm, tn, out_dtype, name, layer=None, n_cols=None):
    m, k = x.shape
    n = n_cols or w.shape[-1]
    if layer is None:
        w_spec = pl.BlockSpec((k, tn), lambda j, i: (0, j))
    else:
        w_spec = pl.BlockSpec((None, k, tn), lambda j, i: (layer, 0, j))
    return pl.pallas_call(
        _mm_kernel,
        grid=(n // tn, m // tm),
        in_specs=[pl.BlockSpec((tm, k), lambda j, i: (i, 0)), w_spec],
        out_specs=pl.BlockSpec((tm, tn), lambda j, i: (i, j)),
        out_shape=jax.ShapeDtypeStruct((m, n), out_dtype),
        scratch_shapes=[pltpu.VMEM((k, tn), BF16)],
        compiler_params=_cparams(("arbitrary", "arbitrary")),
        name=name,
    )(x, w)


def _fcum_kernel(f_ref, b_ref, ccol_ref, *, seq, blk):
    ls = _log_sigmoid(f_ref[...] + b_ref[...])
    r = lax.broadcasted_iota(I32, (blk, blk), 0)
    c = lax.broadcasted_iota(I32, (blk, blk), 1)
    tri = (r >= c).astype(F32)
    carry = jnp.zeros((1, 128), F32)
    for i in range(seq // blk):
        cs = jnp.dot(tri, ls[i * blk:(i + 1) * blk], precision=lax.Precision.HIGHEST,
                     preferred_element_type=F32) + carry
        ccol_ref[i * blk:(i + 1) * blk, :] = cs
        carry = cs[blk - 1:blk, :]


def _forget_cumsum(proj, b_fgate_pad, batch, seq):
    t = batch * seq
    return pl.pallas_call(
        functools.partial(_fcum_kernel, seq=seq, blk=256),
        grid=(batch,),
        in_specs=[pl.BlockSpec((seq, 128), lambda b: (b, 0)),
                  pl.BlockSpec((1, 128), lambda b: (0, 0))],
        out_specs=pl.BlockSpec((seq, 128), lambda b: (b, 0)),
        out_shape=jax.ShapeDtypeStruct((t, 128), F32),
        compiler_params=_cparams(("arbitrary",)),
        name="forget_cumsum",
    )(proj, b_fgate_pad)


ATT_AUG = 2 * HEAD_DIM


def _split3_bf16(c):
    hi = c.astype(BF16).astype(F32)
    r1 = c - hi
    mid = r1.astype(BF16).astype(F32)
    lo = (r1 - mid).astype(BF16).astype(F32)
    return hi, mid, lo


def _attprep_kernel(q_ref, k_ref, v_ref, cc_ref, qa_ref, ka_ref, vb_ref):
    scale = HEAD_DIM ** -0.5 * LOG2_E
    tm = q_ref.shape[0]
    lane = lax.broadcasted_iota(I32, (tm, HEAD_DIM), 1)
    vb_ref[...] = v_ref[...].T.astype(BF16)
    for h in range(ATT_HEADS):
        hs = slice(h * HEAD_DIM, (h + 1) * HEAD_DIM)
        c = jnp.broadcast_to(cc_ref[:, h:h + 1] * LOG2_E, (tm, HEAD_DIM))
        hi, mid, lo = _split3_bf16(c)
        one = jnp.ones_like(c)
        zero = jnp.zeros_like(c)
        q_aug = jnp.where(lane == 0, hi, jnp.where(lane == 1, mid, jnp.where(lane == 2, lo,
                          jnp.where(lane < 6, one, zero))))
        k_aug = jnp.where(lane < 3, one, jnp.where(lane == 3, -hi, jnp.where(lane == 4, -mid,
                          jnp.where(lane == 5, -lo, zero))))
        qa_ref[:, h * ATT_AUG:h * ATT_AUG + HEAD_DIM] = (q_ref[:, hs] * scale).astype(BF16)
        qa_ref[:, h * ATT_AUG + HEAD_DIM:(h + 1) * ATT_AUG] = q_aug.astype(BF16)
        ka_ref[:, h * ATT_AUG:h * ATT_AUG + HEAD_DIM] = k_ref[:, hs].astype(BF16)
        ka_ref[:, h * ATT_AUG + HEAD_DIM:(h + 1) * ATT_AUG] = k_aug.astype(BF16)


def _att_prep(proj, ccol, batch, seq, tm=512):
    t = proj.shape[0]
    ns = seq // tm
    width = ATT_HEADS * HEAD_DIM
    return pl.pallas_call(
        _attprep_kernel,
        grid=(batch, ns),
        in_specs=[pl.BlockSpec((tm, width), lambda b, i: (b * ns + i, 0)),
                  pl.BlockSpec((tm, width), lambda b, i: (b * ns + i, 1)),
                  pl.BlockSpec((tm, width), lambda b, i: (b * ns + i, 2)),
                  pl.BlockSpec((tm, 128), lambda b, i: (b * ns + i, 0))],
        out_specs=[pl.BlockSpec((tm, ATT_HEADS * ATT_AUG), lambda b, i: (b * ns + i, 0)),
                   pl.BlockSpec((tm, ATT_HEADS * ATT_AUG), lambda b, i: (b * ns + i, 0)),
                   pl.BlockSpec((None, width, tm), lambda b, i: (b, 0, i))],
        out_shape=[jax.ShapeDtypeStruct((t, ATT_HEADS * ATT_AUG), BF16),
                   jax.ShapeDtypeStruct((t, ATT_HEADS * ATT_AUG), BF16),
                   jax.ShapeDtypeStruct((batch, width, seq), BF16)],
        compiler_params=_cparams(("arbitrary", "arbitrary")),
        name="att_prep",
    )(proj, proj, proj, ccol)


def _att_kernel(qa_ref, ka_ref, vt_ref, o_ref, m_ref, l_ref, acc_ref, *, tq, tk, nk):
    i = pl.program_id(1)
    j = pl.program_id(2)

    @pl.when(j == 0)
    def _():
        m_ref[...] = jnp.full(m_ref.shape, NEG_INF, F32)
        l_ref[...] = jnp.zeros(l_ref.shape, F32)
        acc_ref[...] = jnp.zeros(acc_ref.shape, F32)

    def step(masked):
        if masked:
            mask = (lax.broadcasted_iota(I32, (tk, tq), 0) <= lax.broadcasted_iota(I32, (tk, tq), 1))
        def scores(h):
            as_ = slice(h * ATT_AUG, (h + 1) * ATT_AUG)
            s = lax.dot_general(ka_ref[:, as_], qa_ref[:, as_], (((1,), (1,)), ((), ())),
                                preferred_element_type=F32)
            return jnp.where(mask, s, NEG_INF) if masked else s

        s_next = scores(0)
        for h in range(ATT_HEADS):
            hs = slice(h * HEAD_DIM, (h + 1) * HEAD_DIM)
            s = s_next
            if h + 1 < ATT_HEADS:
                s_next = scores(h + 1)
            m_prev = m_ref[h:h + 1, :]
            m_new = jnp.maximum(m_prev, jnp.max(s, axis=0, keepdims=True))
            p = jnp.exp2(s - m_new)
            alpha = jnp.exp2(m_prev - m_new)
            l_ref[h:h + 1, :] = alpha * l_ref[h:h + 1, :] + jnp.sum(p, axis=0, keepdims=True)
            acc_ref[hs, :] = alpha * acc_ref[hs, :] + jnp.dot(vt_ref[hs, :], p.astype(BF16),
                                                            preferred_element_type=F32)
            m_ref[h:h + 1, :] = m_new

    @pl.when(j < i)
    def _():
        step(False)

    @pl.when(j == i)
    def _():
        step(True)

    @pl.when(j == nk - 1)
    def _():
        for h in range(ATT_HEADS):
            hs = slice(h * HEAD_DIM, (h + 1) * HEAD_DIM)
            o_ref[:, hs] = (acc_ref[hs, :] / l_ref[h:h + 1, :]).T


def _attention(qa, ka, vt, batch, seq, tq=512):
    t = batch * seq
    nq = seq // tq
    width = ATT_HEADS * HEAD_DIM
    kernel = functools.partial(_att_kernel, tq=tq, tk=tq, nk=nq)
    return pl.pallas_call(
        kernel,
        grid=(batch, nq, nq),
        in_specs=[pl.BlockSpec((tq, ATT_HEADS * ATT_AUG), lambda b, i, j: (b * nq + i, 0)),
                  pl.BlockSpec((tq, ATT_HEADS * ATT_AUG), lambda b, i, j: (b * nq + jnp.minimum(j, i), 0)),
                  pl.BlockSpec((None, width, tq), lambda b, i, j: (b, 0, jnp.minimum(j, i)))],
        out_specs=pl.BlockSpec((tq, width), lambda b, i, j: (b * nq + i, 0)),
        out_shape=jax.ShapeDtypeStruct((t, width), F32),
        scratch_shapes=[pltpu.VMEM((ATT_HEADS, tq), F32),
                        pltpu.VMEM((ATT_HEADS, tq), F32),
                        pltpu.VMEM((width, tq), F32)],
        compiler_params=_cparams(("arbitrary", "arbitrary", "arbitrary")),
        name="fox_attention",
    )(qa, ka, vt)


def _conv_kernel(b_ref, c_ref, h_ref, w_ref, o_ref, carry_ref, *, ts):
    @pl.when(pl.program_id(1) == 0)
    def _():
        carry_ref[...] = jnp.zeros(carry_ref.shape, F32)

    u = c_ref[...] * h_ref[...]
    ext = jnp.concatenate([carry_ref[...], u], axis=0)
    u1 = ext[7:7 + ts]
    u2 = ext[6:6 + ts]
    w = w_ref[...]
    y = w[2:3] * u + w[1:2] * u1 + w[0:1] * u2
    o_ref[...] = b_ref[...] * y
    carry_ref[...] = u[ts - 8:ts]


def _short_conv(proj, conv_w, batch, seq, ts=512):
    t = batch * seq
    ns = seq // ts
    w = conv_w.shape[1]
    return pl.pallas_call(
        functools.partial(_conv_kernel, ts=ts),
        grid=(batch, ns),
        in_specs=[pl.BlockSpec((ts, w), lambda b, i: (b * ns + i, COL512_CONV_B)),
                  pl.BlockSpec((ts, w), lambda b, i: (b * ns + i, COL512_CONV_C)),
                  pl.BlockSpec((ts, w), lambda b, i: (b * ns + i, COL512_CONV_H)),
                  pl.BlockSpec((CONV_K, w), lambda b, i: (0, 0))],
        out_specs=pl.BlockSpec((ts, w), lambda b, i: (b * ns + i, 0)),
        out_shape=jax.ShapeDtypeStruct((t, w), F32),
        scratch_shapes=[pltpu.VMEM((8, w), F32)],
        compiler_params=_cparams(("arbitrary", "arbitrary")),
        name="short_conv",
    )(proj, proj, proj, conv_w)


def _hgrn_kernel(q_ref, f_ref, i_ref, lb_ref, o_ref, st_ref, qs_ref, kk_ref, bc_ref, *, bs):
    c = HGRN_CHUNK

    @pl.when(pl.program_id(1) == 0)
    def _():
        st_ref[...] = jnp.zeros(st_ref.shape, F32)

    z = f_ref[...]
    lb = lb_ref[...]
    ls = _log_sigmoid(z)
    a = jnp.log(lb)
    b = jnp.log1p(-lb) + ls
    log_f = jnp.maximum(a, b) + jnp.log1p(jnp.exp(-jnp.abs(a - b)))
    kk_ref[...] = (1.0 - lb) * jnp.exp(ls - z)
    q = q_ref[...]
    qs_ref[...] = q / (1.0 + jnp.exp(-q))
    r = lax.broadcasted_iota(I32, (bs, bs), 0)
    cc = lax.broadcasted_iota(I32, (bs, bs), 1)
    tri = ((r // c == cc // c) & (r >= cc)).astype(F32)
    bc_ref[...] = jnp.dot(tri, log_f, precision=lax.Precision.HIGHEST, preferred_element_type=F32)

    t_idx = lax.broadcasted_iota(I32, (c, HEAD_DIM), 0)
    ones = jnp.ones((HEAD_DIM, HEAD_DIM), BF16)

    def chunk(n, carry):
        rows = pl.ds(pl.multiple_of(n * c, c), c)
        for h in range(HGRN_HEADS):
            hs = slice(h * HEAD_DIM, (h + 1) * HEAD_DIM)
            qh = qs_ref[rows, hs]
            kh = kk_ref[rows, hs]
            vh = i_ref[rows, hs]
            bh = bc_ref[rows, hs]
            b_last = bh[c - 1:c]
            parts = []
            for s in range(c):
                d = jnp.where(t_idx >= s, bh - bh[s:s + 1], NEG_INF)
                parts.append(qh * jnp.exp(d) * kh[s:s + 1])
            a3 = jnp.concatenate(parts, axis=0).astype(BF16)
            rs = jnp.dot(a3, ones, preferred_element_type=F32)
            o = lax.dot_general((qh * jnp.exp(bh)).astype(BF16), st_ref[h].astype(BF16),
                                (((1,), (1,)), ((), ())), preferred_element_type=F32)
            for s in range(c):
                o = o + rs[s * c:(s + 1) * c] * vh[s:s + 1]
            o_ref[rows, hs] = o
            kd = kh * jnp.exp(b_last - bh)
            kv_t = lax.dot_general(vh.astype(BF16), kd.astype(BF16), (((0,), (0,)), ((), ())),
                                   preferred_element_type=F32)
            st_ref[h] = jnp.exp(b_last) * st_ref[h] + kv_t
        return carry

    lax.fori_loop(0, bs // c, chunk, 0, unroll=4)


def _hgrn2(proj, lb, batch, seq, bs=256):
    t = batch * seq
    ns = seq // bs
    w = HGRN_HEADS * HEAD_DIM
    return pl.pallas_call(
        functools.partial(_hgrn_kernel, bs=bs),
        grid=(batch, ns),
        in_specs=[pl.BlockSpec((bs, w), lambda b, i: (b * ns + i, COL512_HQ)),
                  pl.BlockSpec((bs, w), lambda b, i: (b * ns + i, COL512_HF)),
                  pl.BlockSpec((bs, w), lambda b, i: (b * ns + i, COL512_HI)),
                  pl.BlockSpec((1, w), lambda b, i: (0, 0))],
        out_specs=pl.BlockSpec((bs, w), lambda b, i: (b * ns + i, 0)),
        out_shape=jax.ShapeDtypeStruct((t, w), F32),
        scratch_shapes=[pltpu.VMEM((HGRN_HEADS, HEAD_DIM, HEAD_DIM), F32),
                        pltpu.VMEM((bs, w), F32),
                        pltpu.VMEM((bs, w), F32),
                        pltpu.VMEM((bs, w), F32)],
        compiler_params=_cparams(("arbitrary", "arbitrary")),
        name="hgrn2",
    )(proj, proj, proj, lb)


def _headnorm_kernel(a_ref, c_ref, r_ref, hg_ref, g_ref, o_ref):
    na = a_ref.shape[1] // HEAD_DIM
    nc = c_ref.shape[1] // HEAD_DIM
    nr = r_ref.shape[1] // HEAD_DIM

    def rms(x, g):
        ms = jnp.mean(x * x, axis=-1, keepdims=True)
        return x * lax.rsqrt(ms + RMS_EPS) * g

    col = 0
    for src, n, gated in ((a_ref, na, False), (c_ref, nc, False), (r_ref, nr, True)):
        for h in range(n):
            hs = slice(h * HEAD_DIM, (h + 1) * HEAD_DIM)
            os_ = slice(col, col + HEAD_DIM)
            y = rms(src[:, hs], g_ref[:, os_])
            if gated:
                hg = hg_ref[:, hs]
                y = y * (hg / (1.0 + jnp.exp(-hg)))
            o_ref[:, os_] = y.astype(o_ref.dtype)
            col += HEAD_DIM


def _head_norm(o_att, o_conv, o_rec, proj, g, tm=256):
    t = o_att.shape[0]
    wa, wc, wr = o_att.shape[1], o_conv.shape[1], o_rec.shape[1]
    d = wa + wc + wr
    return pl.pallas_call(
        _headnorm_kernel,
        grid=(t // tm,),
        in_specs=[pl.BlockSpec((tm, wa), lambda i: (i, 0)),
                  pl.BlockSpec((tm, wc), lambda i: (i, 0)),
                  pl.BlockSpec((tm, wr), lambda i: (i, 0)),
                  pl.BlockSpec((tm, wr), lambda i: (i, COL512_HG)),
                  pl.BlockSpec((1, d), lambda i: (0, 0))],
        out_specs=pl.BlockSpec((tm, d), lambda i: (i, 0)),
        out_shape=jax.ShapeDtypeStruct((t, d), BF16),
        compiler_params=_cparams(("arbitrary",)),
        name="head_norm",
    )(o_att, o_conv, o_rec, proj, g)


def _store_token_tiles(ref, base, y, pitch=None):
    n, d = y.shape
    s = d // 128
    for a in range(s):
        ref[pl.ds(base + a, n, stride=pitch or s), :] = y[:, a * 128:(a + 1) * 128]


def _load_token_tiles(ref, base, n, s, pitch=None):
    return [ref[pl.ds(base + a, n, stride=pitch or s), :] for a in range(s)]


def _layer_norm_rows(z, g, b):
    mu = jnp.mean(z, axis=-1, keepdims=True)
    zc = z - mu
    var = jnp.mean(zc * zc, axis=-1, keepdims=True)
    return zc * lax.rsqrt(var + LN_EPS) * g + b


def _addln_kernel(x_ref, y_ref, g_ref, b_ref, o_ref, ot_ref, *, alpha):
    y = _layer_norm_rows(alpha * x_ref[...] + y_ref[...], g_ref[...], b_ref[...])
    o_ref[...] = y
    _store_token_tiles(ot_ref, 0, y)


def _add_ln(x, y, g, b, alpha, tm=256):
    t, d = x.shape
    s = d // 128
    row = pl.BlockSpec((tm, d), lambda i: (i, 0))
    vec = pl.BlockSpec((1, d), lambda i: (0, 0))
    return pl.pallas_call(
        functools.partial(_addln_kernel, alpha=alpha),
        grid=(t // tm,),
        in_specs=[row, row, vec, vec],
        out_specs=[row, pl.BlockSpec((tm * s, 128), lambda i: (i, 0))],
        out_shape=[jax.ShapeDtypeStruct((t, d), F32),
                   jax.ShapeDtypeStruct((t * s, 128), F32)],
        compiler_params=_cparams(("arbitrary",)),
        name="add_layernorm",
    )(x, y, g, b)


def _router_kernel(x_ref, w_ref, b_ref, idx_ref, wt_ref):
    logits = jnp.dot(x_ref[...], w_ref[...], precision=lax.Precision.HIGHEST,
                     preferred_element_type=F32) + b_ref[...]
    lane = lax.broadcasted_iota(I32, logits.shape, 1)
    logits = jnp.where(lane < N_EXPERTS, logits, NEG_INF)
    vals, idxs = [], []
    for _ in range(TOP_K):
        m = jnp.max(logits, axis=-1, keepdims=True)
        sel = jnp.min(jnp.where(logits == m, lane, 128), axis=-1, keepdims=True)
        vals.append(m)
        idxs.append(sel)
        logits = jnp.where(lane == sel, NEG_INF, logits)
    es = [jnp.exp(v - vals[0]) for v in vals]
    tot = es[0] + es[1] + es[2] + es[3]
    idx_out = jnp.zeros(logits.shape, I32)
    wt_out = jnp.zeros(logits.shape, F32)
    for k in range(TOP_K):
        idx_out = jnp.where(lane == k, idxs[k], idx_out)
        wt_out = jnp.where(lane == k, es[k] / tot, wt_out)
    idx_ref[...] = idx_out
    wt_ref[...] = wt_out


def _router(x, w_pad, b_pad, tm=512):
    t, d = x.shape
    return pl.pallas_call(
        _router_kernel,
        grid=(t // tm,),
        in_specs=[pl.BlockSpec((tm, d), lambda i: (i, 0)),
                  pl.BlockSpec((d, 128), lambda i: (0, 0)),
                  pl.BlockSpec((1, 128), lambda i: (0, 0))],
        out_specs=[pl.BlockSpec((tm, 128), lambda i: (i, 0)),
                   pl.BlockSpec((tm, 128), lambda i: (i, 0))],
        out_shape=[jax.ShapeDtypeStruct((t, 128), I32),
                   jax.ShapeDtypeStruct((t, 128), F32)],
        compiler_params=_cparams(("arbitrary",)),
        name="router_top4",
    )(x, w_pad, b_pad)


MOE_UNIT = 128
MOE_TM = 256
MOE_TILES_PER_ITER = 4
MOE_TN = 256
MOE_CAP = 1280
MOE_ROW_UNROLL = 8
MOE_VMEM_PITCH = 20
MOE_VMEM_PITCH_ALIGN = 4


MOE_PLAN_GROUPS = 128
MOE_CAP_SHIFT = 26
MOE_CAP_RECIP = -(-(1 << MOE_CAP_SHIFT) // MOE_CAP)


def _plan_kernel(idx_ref, gb_ref, tbl_ref, run_ref, hi_ref, lo_ref, cnt_ref, *, n_tokens):
    i = pl.program_id(0)
    tm = idx_ref.shape[0]

    @pl.when(i == 0)
    def _():
        run_ref[...] = jnp.zeros(run_ref.shape, F32)
        hi_ref[...] = jnp.zeros(hi_ref.shape, F32)
        lo_ref[...] = jnp.zeros(lo_ref.shape, F32)
        cnt_ref[...] = jnp.zeros(cnt_ref.shape, F32)

    idx = idx_ref[...]
    lane = lax.broadcasted_iota(I32, (tm, 128), 1)
    tri = (lax.broadcasted_iota(I32, (tm, tm), 0) >= lax.broadcasted_iota(I32, (tm, tm), 1)).astype(BF16)
    g_base = gb_ref[...].astype(F32)
    run = run_ref[...]
    tok = i * tm + lax.broadcasted_iota(I32, (tm, 1), 0)
    col_g = lax.broadcasted_iota(I32, (tm, MOE_PLAN_GROUPS), 1)
    col_r = lax.broadcasted_iota(I32, (tm, MOE_CAP), 1)
    tn = (((0,), (0,)), ((), ()))
    for k in range(TOP_K):
        onehot = (lane == idx[:, k:k + 1]).astype(F32)
        prefix = jnp.dot(tri, onehot.astype(BF16), preferred_element_type=F32)
        rank = jnp.sum(onehot * (run + prefix - 1.0), axis=-1, keepdims=True).astype(I32)
        first = jnp.sum(onehot * g_base, axis=-1, keepdims=True).astype(I32)
        run = run + prefix[tm - 1:tm, :]
        sub = (rank * MOE_CAP_RECIP) >> MOE_CAP_SHIFT
        grp = first + sub
        row = rank - sub * MOE_CAP
        dst = k * n_tokens + tok
        sel_g = (col_g == grp).astype(BF16)
        sel_r = col_r == row
        d_hi = (dst >> 8).astype(F32)
        d_lo = (dst & 255).astype(F32)
        cnt_ref[...] += lax.dot_general(sel_g, sel_r.astype(BF16), tn, preferred_element_type=F32)
        hi_ref[...] += lax.dot_general(sel_g, jnp.where(sel_r, d_hi, 0.0).astype(BF16), tn,
                                       preferred_element_type=F32)
        lo_ref[...] += lax.dot_general(sel_g, jnp.where(sel_r, d_lo, 0.0).astype(BF16), tn,
                                       preferred_element_type=F32)
    run_ref[...] = run

    @pl.when(i == pl.num_programs(0) - 1)
    def _():
        trash = TOP_K * n_tokens + (lax.broadcasted_iota(I32, tbl_ref.shape, 1) & (MOE_UNIT - 1))
        val = hi_ref[...].astype(I32) * 256 + lo_ref[...].astype(I32)
        tbl_ref[...] = jnp.where(cnt_ref[...] > 0.5, val, trash)


def _moe_plan(idx, n_tokens, tm=512):
    n_pairs = n_tokens * TOP_K
    g_max = N_EXPERTS + -(-n_pairs // MOE_CAP)
    assert g_max <= MOE_PLAN_GROUPS and n_pairs <= (1 << 15) and TOP_K * n_tokens + MOE_UNIT <= (1 << 16)
    assert all((r * MOE_CAP_RECIP) >> MOE_CAP_SHIFT == r // MOE_CAP for r in range(1 << 15))
    counts = jnp.sum((idx[:, :TOP_K, None] == jnp.arange(N_EXPERTS, dtype=I32)).astype(I32), axis=(0, 1))
    n_grp = (counts + MOE_CAP - 1) // MOE_CAP
    g_end = jnp.cumsum(n_grp)
    g_base = g_end - n_grp
    gb_pad = jnp.zeros((1, 128), I32).at[0, :N_EXPERTS].set(g_base)
    acc = pltpu.VMEM((MOE_PLAN_GROUPS, MOE_CAP), F32)
    table = pl.pallas_call(
        functools.partial(_plan_kernel, n_tokens=n_tokens),
        grid=(n_tokens // tm,),
        in_specs=[pl.BlockSpec((tm, 128), lambda i: (i, 0)),
                  pl.BlockSpec((1, 128), lambda i: (0, 0))],
        out_specs=pl.BlockSpec((MOE_PLAN_GROUPS, MOE_CAP), lambda i: (0, 0)),
        out_shape=jax.ShapeDtypeStruct((MOE_PLAN_GROUPS, MOE_CAP), I32),
        scratch_shapes=[pltpu.VMEM((1, 128), F32), acc, acc, acc],
        compiler_params=_cparams(("arbitrary",)),
        name="moe_plan",
    )(idx, gb_pad)
    gid = jnp.arange(g_max, dtype=I32)
    n_used = g_end[-1]
    g_exp = jnp.sum((jnp.minimum(gid, n_used - 1)[:, None] >= g_end[None, :]).astype(I32), axis=1)
    g_rows = jnp.clip(counts[g_exp] - (gid - g_base[g_exp]) * MOE_CAP, 0, MOE_CAP)
    g_rows = jnp.where(gid < n_used, g_rows, 0)
    g_units = ((g_rows + MOE_UNIT - 1) // MOE_UNIT).astype(I32)
    return table, g_exp, g_units, g_max


def _moe_kernel(ge_ref, gt_ref, tbl_hbm, x_hbm, w1g_ref, w1u_ref, b1g_ref, b1u_ref, w2_ref, b2_ref,
                out_hbm, tbl_smem, xbuf, xb, yacc, ystage, w1g_b, w1u_b, w2_b, sem_tbl, sem_in, sem_out,
                *, n_tokens, n_steps):
    g = pl.program_id(0)
    j = pl.program_id(1)
    n_groups = pl.num_programs(0)
    nt = gt_ref[g]
    nt_prev = gt_ref[jnp.maximum(g - 1, 0)]
    g_next = jnp.minimum(g + 1, n_groups - 1)
    tm = MOE_UNIT
    d = yacc.shape[1]
    s = d // 128
    tbase = (g & 1) * MOE_CAP

    pitch = MOE_VMEM_PITCH

    def token_rows(r, n=1):
        return pl.ds(pl.multiple_of(r * s, s), n * s)

    def vmem_token(r):
        return pl.ds(pl.multiple_of(r * pitch, MOE_VMEM_PITCH_ALIGN), s)

    def start_gather(grp, n_tiles):
        base = (grp & 1) * MOE_CAP
        cp = pltpu.make_async_copy(tbl_hbm.at[grp], tbl_smem.at[pl.ds(base, MOE_CAP)], sem_tbl)
        cp.start()
        cp.wait()

        def row_copy_in(r, prio):
            tok = tbl_smem[base + r] & (n_tokens - 1)
            pltpu.make_async_copy(x_hbm.at[token_rows(tok)], xbuf.at[vmem_token(r)], sem_in).start(priority=prio)

        for_rows(0, n_tiles * tm, row_copy_in)

    def row_copy_out(r, slot):
        src = ystage.at[vmem_token(slot * tm + (r & (tm - 1)))]
        return pltpu.make_async_copy(src, out_hbm.at[token_rows(tbl_smem[tbase + r])], sem_out.at[slot])

    def wait_tile_out(slot):
        pltpu.make_async_copy(ystage.at[token_rows(0, tm)], out_hbm.at[token_rows(0, tm)],
                              sem_out.at[slot]).wait()

    def drain_scatter(n_tiles):
        @pl.when(n_tiles >= 2)
        def _():
            wait_tile_out(n_tiles & 1)

        wait_tile_out((n_tiles - 1) & 1)

    def for_rows(first, n_rows, fn):
        def body(i, c):
            for u in range(MOE_ROW_UNROLL):
                fn(first + i * MOE_ROW_UNROLL + u, u % 2)
            return c

        lax.fori_loop(0, n_rows // MOE_ROW_UNROLL, body, 0)

    def for_tiles(fn):
        def body(m, c):
            fn(m)
            return c

        lax.fori_loop(0, nt, body, 0)

    def tile_rows(m):
        return pl.ds(pl.multiple_of(m * tm, tm), tm)

    @pl.when((g == 0) & (j == 0))
    def _():
        ystage[pl.ds(0, tm * s), :] = jnp.zeros((tm * s, 128), F32)
        cp = pltpu.make_async_copy(ystage.at[token_rows(0, tm)],
                                   out_hbm.at[token_rows(TOP_K * n_tokens, tm)], sem_out.at[0])
        cp.start()
        cp.wait()
        start_gather(g, nt)

    @pl.when((nt == 0) & (j == 0) & (g > 0) & (nt_prev > 0))
    def _():
        drain_scatter(nt_prev)

    @pl.when(nt > 0)
    def _():
        @pl.when(j == 0)
        def _():
            def init_acc(m):
                yacc[tile_rows(m), :] = jnp.broadcast_to(b2_ref[...], (tm, d))

            for_tiles(init_acc)

            def wait_rows(m):
                pltpu.make_async_copy(x_hbm.at[token_rows(0, tm)], xbuf.at[token_rows(0, tm)], sem_in).wait()

            for_tiles(wait_rows)

            def to_rows(m):
                parts = _load_token_tiles(xbuf, pl.multiple_of(m * tm * pitch, tm * pitch), tm, s, pitch)
                for a in range(s):
                    xb[tile_rows(m), a * 128:(a + 1) * 128] = parts[a].astype(BF16)

            for_tiles(to_rows)

            @pl.when((g + 1 < n_groups) & (gt_ref[g_next] > 0))
            def _():
                start_gather(g_next, gt_ref[g_next])

        w1g_b[...] = w1g_ref[...].astype(BF16)
        w1u_b[...] = w1u_ref[...].astype(BF16)
        w2_b[...] = w2_ref[...].astype(BF16)

        def mm_tile(m, n_rows):
            rows = pl.ds(pl.multiple_of(m * MOE_TM, MOE_TM), n_rows)
            x = xb[rows, :]
            hg = jnp.dot(x, w1g_b[...], preferred_element_type=F32) + b1g_ref[...]
            hu = jnp.dot(x, w1u_b[...], preferred_element_type=F32) + b1u_ref[...]
            gate = jnp.minimum(hg, SWIGLU_LIMIT)
            up = jnp.clip(hu, -SWIGLU_LIMIT, SWIGLU_LIMIT)
            act = (up + 1.0) * gate / (1.0 + jnp.exp(-SWIGLU_ALPHA * gate))
            yacc[rows, :] += jnp.dot(act.astype(BF16), w2_b[...], preferred_element_type=F32)

        def multi(p, c):
            for u in range(MOE_TILES_PER_ITER):
                mm_tile(p * MOE_TILES_PER_ITER + u, MOE_TM)
            return c

        n_full = nt // (MOE_TM // MOE_UNIT)
        n_multi = n_full // MOE_TILES_PER_ITER
        lax.fori_loop(0, n_multi, multi, 0)

        def single(m, c):
            mm_tile(m, MOE_TM)
            return c

        lax.fori_loop(n_multi * MOE_TILES_PER_ITER, n_full, single, 0)

        @pl.when(nt > n_full * (MOE_TM // MOE_UNIT))
        def _():
            mm_tile(n_full, MOE_UNIT)

        @pl.when(j == n_steps - 1)
        def _():
            @pl.when(g > 0)
            def _():
                drain_scatter(nt_prev)

            def emit(m):
                slot = m & 1

                @pl.when(m >= 2)
                def _():
                    wait_tile_out(slot)

                _store_token_tiles(ystage, pl.multiple_of(slot * tm * pitch, tm * pitch), yacc[tile_rows(m), :],
                                   pitch)
                for_rows(m * tm, tm, lambda r, prio: row_copy_out(r, slot).start(priority=prio))

            for_tiles(emit)


def _moe_ffn(xt, table, g_exp, g_units, g_max, w1, b1, w2, b2, layer, n_tokens):
    d = w2.shape[3]
    s = d // 128
    d_ff = w2.shape[2]
    n_steps = d_ff // MOE_TN
    tn = MOE_TN

    def jj(g, j, gt):
        return jnp.where(gt[g] > 0, j, n_steps - 1)

    grid_spec = pltpu.PrefetchScalarGridSpec(
        num_scalar_prefetch=2,
        grid=(g_max, n_steps),
        in_specs=[
            pl.BlockSpec(memory_space=pl.ANY),
            pl.BlockSpec(memory_space=pl.ANY),
            pl.BlockSpec((None, None, d, tn), lambda g, j, ge, gt: (layer, ge[g], 0, jj(g, j, gt))),
            pl.BlockSpec((None, None, d, tn), lambda g, j, ge, gt: (layer, ge[g], 0, jj(g, j, gt) + n_steps)),
            pl.BlockSpec((None, None, 1, tn), lambda g, j, ge, gt: (layer, ge[g], 0, jj(g, j, gt))),
            pl.BlockSpec((None, None, 1, tn), lambda g, j, ge, gt: (layer, ge[g], 0, jj(g, j, gt) + n_steps)),
            pl.BlockSpec((None, None, tn, d), lambda g, j, ge, gt: (layer, ge[g], jj(g, j, gt), 0)),
            pl.BlockSpec((None, None, 1, d), lambda g, j, ge, gt: (layer, ge[g], 0, 0)),
        ],
        out_specs=pl.BlockSpec(memory_space=pl.ANY),
        scratch_shapes=[
            pltpu.SMEM((2 * MOE_CAP,), I32),
            pltpu.VMEM((MOE_CAP * MOE_VMEM_PITCH, 128), F32),
            pltpu.VMEM((MOE_CAP, d), BF16),
            pltpu.VMEM((MOE_CAP, d), F32),
            pltpu.VMEM((2 * MOE_UNIT * MOE_VMEM_PITCH, 128), F32),
            pltpu.VMEM((d, tn), BF16),
            pltpu.VMEM((d, tn), BF16),
            pltpu.VMEM((tn, d), BF16),
            pltpu.SemaphoreType.DMA,
            pltpu.SemaphoreType.DMA,
            pltpu.SemaphoreType.DMA((2,)),
        ],
    )
    return pl.pallas_call(
        functools.partial(_moe_kernel, n_tokens=n_tokens, n_steps=n_steps),
        grid_spec=grid_spec,
        out_shape=jax.ShapeDtypeStruct(((TOP_K * n_tokens + MOE_UNIT) * s, 128), F32),
        compiler_params=_cparams(("arbitrary", "arbitrary")),
        name="moe_ffn",
    )(g_exp, g_units, table, xt, w1, w1, b1, b1, w2, b2)


def _combine_kernel(y0_ref, y1_ref, y2_ref, y3_ref, wt_ref, x_ref, g_ref, b_ref, o_ref, ob_ref, z_ref, *, alpha):
    tm, d = x_ref.shape
    s = d // 128
    wt = wt_ref[...]
    slots = [_load_token_tiles(y_ref, 0, tm, s) for y_ref in (y0_ref, y1_ref, y2_ref, y3_ref)]
    for a in range(s):
        cs = slice(a * 128, (a + 1) * 128)
        y = (wt[:, 0:1] * slots[0][a] + wt[:, 1:2] * slots[1][a]
             + wt[:, 2:3] * slots[2][a] + wt[:, 3:4] * slots[3][a])
        z_ref[:, cs] = alpha * x_ref[:, cs] + y
    z = _layer_norm_rows(z_ref[...], g_ref[...], b_ref[...])
    o_ref[...] = z
    ob_ref[...] = z.astype(BF16)


def _combine_ln(y4, wts, x, g, b, alpha, tm=256):
    t, d = x.shape
    s = d // 128
    nb = t // tm
    row = pl.BlockSpec((tm, d), lambda i: (i, 0))
    vec = pl.BlockSpec((1, d), lambda i: (0, 0))
    slot = [pl.BlockSpec((tm * s, 128), functools.partial(lambda i, k: (k * nb + i, 0), k=k))
            for k in range(TOP_K)]
    return pl.pallas_call(
        functools.partial(_combine_kernel, alpha=alpha),
        grid=(nb,),
        in_specs=slot + [pl.BlockSpec((tm, 128), lambda i: (i, 0)), row, vec, vec],
        out_specs=[row, row],
        out_shape=[jax.ShapeDtypeStruct((t, d), F32), jax.ShapeDtypeStruct((t, d), BF16)],
        scratch_shapes=[pltpu.VMEM((tm, d), F32)],
        compiler_params=_cparams(("arbitrary",)),
        name="combine_layernorm",
    )(y4, y4, y4, y4, wts, x, g, b)


def kernel(x, w_in, b_fgate, conv_w, lower_bounds, head_norm_g, w_out, ln1_g, ln1_b, w_router, b_router,
           w1, b1, w2, b2, ln2_g, ln2_b):
    batch, seq, d = x.shape
    depth = w_in.shape[0]
    t = batch * seq
    alpha = (2 * depth) ** 0.25

    lb = jnp.cumsum(jax.nn.softmax(lower_bounds.astype(F32), axis=0), axis=0)
    lb = lb - lb[0]
    b1r = b1.reshape(depth, N_EXPERTS, 1, b1.shape[-1])
    b2r = b2.reshape(depth, N_EXPERTS, 1, b2.shape[-1])

    xf = x.reshape(t, d)
    xb = xf.astype(BF16)
    for l in range(depth):
        w_f = jnp.zeros((d, 128), F32).at[:, :ATT_HEADS].set(w_in[l, :, ATT_COLS:REST_COL0])
        qkv = _matmul(xb, w_in, 512, 1536, F32, "in_proj_att", layer=l, n_cols=ATT_COLS)
        f_logit = _matmul(xb, w_f, 512, 128, F32, "in_proj_forget")
        proj = _matmul(xb, w_in[l, :, REST_COL0:], 512, 1792, F32, "in_proj_rest")
        bf_pad = jnp.zeros((1, 128), F32).at[0, :ATT_HEADS].set(b_fgate[l])
        ccol = _forget_cumsum(f_logit, bf_pad, batch, seq)
        qa, ka, vt = _att_prep(qkv, ccol, batch, seq)
        o_att = _attention(qa, ka, vt, batch, seq)
        o_conv = _short_conv(proj, conv_w[l], batch, seq)
        o_rec = _hgrn2(proj, lb[l][None, :], batch, seq)
        o = _head_norm(o_att, o_conv, o_rec, proj, head_norm_g[l][None, :])
        mix = _matmul(o, w_out, 512, 1024, F32, "out_proj", layer=l)
        x1, x1t = _add_ln(xf, mix, ln1_g[l][None, :], ln1_b[l][None, :], alpha)

        wr_pad = jnp.zeros((d, 128), F32).at[:, :N_EXPERTS].set(w_router[l])
        br_pad = jnp.zeros((1, 128), F32).at[0, :N_EXPERTS].set(b_router[l])
        idx, wts = _router(x1, wr_pad, br_pad)
        table, g_exp, g_units, g_max = _moe_plan(idx, t)
        y4 = _moe_ffn(x1t, table, g_exp, g_units, g_max, w1, b1r, w2, b2r, l, t)
        xf, xb = _combine_ln(y4, wts, x1, ln2_g[l][None, :], ln2_b[l][None, :], alpha)
    return xf.reshape(batch, seq, d)
```
